```python
import jax, jax.numpy as jnp
from jax import lax
import numpy as np

D_MODEL = 1024
BATCH = 8
SEQ = 2048
DEPTH = 2
DEC_BATCH = 128
DEC_SEQ = 8
PAST_LEN = 2048
PAGE_SIZE = 128

N_MIXERS = 2
N_A = (DEPTH + 1) // 2
N_B = DEPTH // 2
RW_HEAD = 64
RW_HEADS = D_MODEL // RW_HEAD
RW_DECAY_LORA = 64
RW_AAA_LORA = 64
RW_GATE_LORA = 160
RW_GN_EPS = 64e-5
ATT_GROUPS = ((128, 1), (512, 4), (2048, 16))
N_GROUPS = 3
ATT_HEADS = 8
ATT_HEAD_DIM = 64
ATT_WIDTH = ATT_HEADS * ATT_HEAD_DIM
ATT_SCALE = ATT_HEAD_DIM ** -0.5
ROT_DIM = ATT_HEAD_DIM // 4
ROPE_THETA = 500000.0
D_FF = 2816
CONV_W = 3
NORM_EPS = 1e-6
F32 = jnp.float32

kernel_name = 'hybrid_rwkv7_dilated_swa_convffn_step'


def rmsnorm(x, g):
    xf = x.astype(F32)
    y = xf * lax.rsqrt(jnp.mean(xf * xf, axis=-1, keepdims=True) + NORM_EPS)
    return (y * g.astype(F32)).astype(x.dtype)


def rope(x, pos):
    half = ROT_DIM // 2
    inv = jnp.power(ROPE_THETA, -jnp.arange(half, dtype=F32) / half)
    ang = pos.astype(F32)[:, None] * inv[None, :]
    cos = jnp.cos(ang)[None, :, None, :]
    sin = jnp.sin(ang)[None, :, None, :]
    xf = x.astype(F32)
    x1, x2 = xf[..., :half], xf[..., half:ROT_DIM]
    out = jnp.concatenate([x1 * cos - x2 * sin, x2 * cos + x1 * sin, xf[..., ROT_DIM:]], axis=-1)
    return out.astype(x.dtype)


def rwkv7_mix(xn, shift_prev, s0, p):
    (mu, wr, wk, wv, wo, w0, w1, w2, a0, a1, a2, g1, g2, k_k, k_a, r_k, lnx_g, lnx_b) = p
    b, t, d = xn.shape
    h, n = RW_HEADS, RW_HEAD
    prev = jnp.concatenate([shift_prev[:, None, :].astype(xn.dtype), xn[:, :-1]], axis=1)
    xx = prev - xn
    xr, xw, xk, xv, xa, xg = [xn + xx * mu[i] for i in range(6)]
    r = xr @ wr
    k = xk @ wk
    v = xv @ wv
    w = -jax.nn.softplus(-(w0 + jnp.tanh(xw @ w1) @ w2)) - 0.5
    a = jax.nn.sigmoid(a0 + (xa @ a1) @ a2)
    g = jax.nn.sigmoid(xg @ g1) @ g2
    heads = lambda z: z.reshape(b, t, h, n).astype(F32)
    kk = heads(k * k_k)
    kk = kk / jnp.maximum(jnp.linalg.norm(kk, axis=-1, keepdims=True), 1e-12)
    k = k * (1 + (a - 1) * k_a)
    r_h, k_h, v_h, a_h = heads(r), heads(k), heads(v), heads(a)
    decay = jnp.exp(-jnp.exp(heads(w)))

    def step(S, inp):
        r_t, d_t, k_t, v_t, kk_t, a_t = inp
        sk = jnp.einsum('bhvk,bhk->bhv', S, kk_t)
        S = (S * d_t[:, :, None, :] - sk[..., None] * (kk_t * a_t)[:, :, None, :]
             + v_t[..., None] * k_t[:, :, None, :])
        return S, jnp.einsum('bhvk,bhk->bhv', S, r_t)

    tm = lambda z: jnp.swapaxes(z, 0, 1)
    s_fin, ys = lax.scan(step, s0.astype(F32),
                         (tm(r_h), tm(decay), tm(k_h), tm(v_h), tm(kk), tm(a_h)))
    y = tm(ys)
    mean = jnp.mean(y, axis=-1, keepdims=True)
    var = jnp.mean(jnp.square(y - mean), axis=-1, keepdims=True)
    y = ((y - mean) * lax.rsqrt(var + RW_GN_EPS)).reshape(b, t, d) * lnx_g + lnx_b
    bonus = jnp.sum(r_h * k_h * r_k, axis=-1, keepdims=True) * v_h
    y = (y + bonus.reshape(b, t, d)).astype(xn.dtype)
    out = (y * g) @ wo
    return out, xn[:, -1], s_fin.astype(s0.dtype)


def dilated_group_prompt(q, k, v, window, dil):
    b, s, h, e = q.shape
    blk = window // dil
    n_sub = s // dil
    nb = -(-n_sub // blk)
    lp = nb * blk

    def sub(z):
        z = z.reshape(b, n_sub, dil, h, e).transpose(0, 2, 1, 3, 4)
        return jnp.pad(z, ((0, 0), (0, 0), (0, lp - n_sub), (0, 0), (0, 0)))

    def band(z):
        z = jnp.pad(sub(z), ((0, 0), (0, 0), (blk, 0), (0, 0), (0, 0))).reshape(b, dil, nb + 1, blk, h, e)
        return jnp.concatenate([z[:, :, :-1], z[:, :, 1:]], axis=3)

    qb = sub(q).reshape(b, dil, nb, blk, h, e)
    kb, vb = band(k), band(v)
    sc = jnp.einsum('bdnqhe,bdnkhe->bdnqhk', qb, kb, preferred_element_type=F32) * ATT_SCALE
    qi = jnp.arange(blk)[:, None]
    kj = jnp.arange(2 * blk)[None, :]
    dist = qi + blk - kj
    kidx = jnp.arange(nb)[:, None, None] * blk + kj[None] - blk
    valid = (dist >= 0)[None] & (dist <= blk)[None] & (kidx >= 0)
    sc = jnp.where(valid[None, None, :, :, None, :], sc, -jnp.inf)
    lse = jax.nn.logsumexp(sc, axis=-1)
    pr = jnp.exp(sc - lse[..., None])
    o = jnp.einsum('bdnqhk,bdnkhe->bdnqhe', pr, vb.astype(F32))
    o = o.reshape(b, dil, lp, h, e)[:, :, :n_sub].transpose(0, 2, 1, 3, 4).reshape(b, s, h, e)
    lse = lse.reshape(b, dil, lp, h)[:, :, :n_sub].transpose(0, 2, 1, 3).reshape(b, s, h)
    return o, lse


def dilated_group_sample(q, k_all, v_all, window, dil, offset):
    t = q.shape[1]
    nk = window // dil + 1
    idx = offset + jnp.arange(t)[:, None] - dil * jnp.arange(nk)[None, :]
    valid = idx >= 0
    idx = jnp.maximum(idx, 0)
    kg = jnp.take(k_all, idx, axis=1)
    vg = jnp.take(v_all, idx, axis=1)
    sc = jnp.einsum('bthe,btjhe->bthj', q, kg, preferred_element_type=F32) * ATT_SCALE
    sc = jnp.where(valid[None, :, None, :], sc, -jnp.inf)
    lse = jax.nn.logsumexp(sc, axis=-1)
    pr = jnp.exp(sc - lse[..., None])
    o = jnp.einsum('bthj,btjhe->bthe', pr, vg.astype(F32))
    return o, lse


def dilated_attention(xn, pos, bufs, wqkv, wo):
    b, t, _ = xn.shape
    qkv = (xn @ wqkv).reshape(b, t, 3, N_GROUPS, ATT_HEADS, ATT_HEAD_DIM)
    outs, lses, new = [], [], []
    for gi in range(N_GROUPS):
        window, dil = ATT_GROUPS[gi]
        q = rope(qkv[:, :, 0, gi], pos)
        k = rope(qkv[:, :, 1, gi], pos)
        v = qkv[:, :, 2, gi]
        if bufs is None:
            o, lse = dilated_group_prompt(q, k, v, window, dil)
            keep = min(window, t)
            new.append(jnp.stack([k[:, t - keep:], v[:, t - keep:]], axis=2))
        else:
            buf = bufs[gi].astype(k.dtype)
            k_all = jnp.concatenate([buf[:, :, 0], k], axis=1)
            v_all = jnp.concatenate([buf[:, :, 1], v], axis=1)
            o, lse = dilated_group_sample(q, k_all, v_all, window, dil, buf.shape[1])
            new.append(jnp.stack([k, v], axis=2))
        outs.append(o)
        lses.append(lse)
    wts = jax.nn.softmax(jnp.stack(lses), axis=0)
    o = jnp.sum(wts[..., None] * jnp.stack(outs), axis=0)
    y = o.reshape(b, t, ATT_WIDTH).astype(xn.dtype) @ wo
    return y, new


def conv_ffn(xn, prev, wup, cw, cb, wdown):
    t = xn.shape[1]
    hid = xn @ wup
    hp = jnp.concatenate([prev.astype(hid.dtype), hid], axis=1)
    c = cb
    for j in range(CONV_W):
        c = c + hp[:, j:j + t] * cw[j]
    gate, val = jnp.split(c, 2, axis=-1)
    out = (jax.nn.gelu(gate, approximate=True) * val) @ wdown
    return out, hp[:, t:]


def run_trunk(x, pos, shift0, wkv0, kv_bufs, conv0, norm_g, rw, at_wqkv, at_wo, ff):
    new_shift, new_wkv, new_conv = [], [], []
    new_kv = [[], [], []]
    ia = 0
    ib = 0
    for layer in range(DEPTH):
        g = norm_g[layer]
        hn = rmsnorm(x, g[0])
        if layer % N_MIXERS == 0:
            mo, sh, st = rwkv7_mix(hn, shift0[ia], wkv0[ia], [p[ia] for p in rw])
            new_shift.append(sh)
            new_wkv.append(st)
            ia += 1
        else:
            bufs = None if kv_bufs is None else [buf[ib] for buf in kv_bufs]
            mo, kvs = dilated_attention(hn, pos, bufs, at_wqkv[ib], at_wo[ib])
            for gi in range(N_GROUPS):
                new_kv[gi].append(kvs[gi])
            ib += 1
        x = x + rmsnorm(mo, g[1])
        hn = rmsnorm(x, g[2])
        fo, cs = conv_ffn(hn, conv0[layer], ff[0][layer], ff[1][layer], ff[2][layer], ff[3][layer])
        new_conv.append(cs)
        x = x + rmsnorm(fo, g[3])
    kv_out = [jnp.stack(kv) for kv in new_kv]
    return x, jnp.stack(new_shift), jnp.stack(new_wkv), kv_out, jnp.stack(new_conv)


def setup_inputs(seed: int = 0) -> dict:
    key = jax.random.key(seed)
    ks = iter(jax.random.split(key, 64))
    nrm = lambda shape, scale=1.0: jax.random.normal(next(ks), shape, F32) * scale
    D = D_MODEL
    H, N = RW_HEADS, RW_HEAD
    F2 = 2 * D_FF
    wl = [min(w, PAST_LEN) for (w, _) in ATT_GROUPS]
    kvshape = lambda L: (N_B, DEC_BATCH, L, 2, ATT_HEADS, ATT_HEAD_DIM)
    return {
        'x_prompt': nrm((BATCH, SEQ, D)),
        'x_sample': nrm((DEC_BATCH, DEC_SEQ, D)),
        'state_shift': nrm((N_A, DEC_BATCH, D)),
        'state_wkv': nrm((N_A, DEC_BATCH, H, N, N), 0.3),
        'cache_kv_w128': nrm(kvshape(wl[0])),
        'cache_kv_w512': nrm(kvshape(wl[1])),
        'cache_kv_w2048': nrm(kvshape(wl[2])),
        'state_conv': nrm((DEPTH, DEC_BATCH, CONV_W - 1, F2)),
        'norm_g': 1.0 + nrm((DEPTH, 4, D), 0.05),
        'rw_mu': jax.random.uniform(next(ks), (N_A, 6, D), F32),
        'rw_wr': nrm((N_A, D, D), D ** -0.5),
        'rw_wk': nrm((N_A, D, D), D ** -0.5),
        'rw_wv': nrm((N_A, D, D), D ** -0.5),
        'rw_wo': nrm((N_A, D, D), D ** -0.5),
        'rw_w0': nrm((N_A, D), 0.5),
        'rw_w1': nrm((N_A, D, RW_DECAY_LORA), D ** -0.5),
        'rw_w2': nrm((N_A, RW_DECAY_LORA, D), 0.3 * RW_DECAY_LORA ** -0.5),
        'rw_a0': nrm((N_A, D), 0.5),
        'rw_a1': nrm((N_A, D, RW_AAA_LORA), D ** -0.5),
        'rw_a2': nrm((N_A, RW_AAA_LORA, D), 0.3 * RW_AAA_LORA ** -0.5),
        'rw_g1': nrm((N_A, D, RW_GATE_LORA), D ** -0.5),
        'rw_g2': nrm((N_A, RW_GATE_LORA, D), RW_GATE_LORA ** -0.5),
        'rw_kk': 0.85 + nrm((N_A, D), 0.05),
        'rw_ka': 1.0 + nrm((N_A, D), 0.05),
        'rw_rk': nrm((N_A, H, N), 0.1),
        'rw_lnx_g': 1.0 + nrm((N_A, D), 0.05),
        'rw_lnx_b': nrm((N_A, D), 0.02),
        'at_wqkv': nrm((N_B, D, 3 * N_GROUPS * ATT_WIDTH), D ** -0.5),
        'at_wo': nrm((N_B, ATT_WIDTH, D), ATT_WIDTH ** -0.5),
        'ff_wup': nrm((DEPTH, D, F2), D ** -0.5),
        'ff_conv_w': nrm((DEPTH, CONV_W, F2), CONV_W ** -0.5),
        'ff_conv_b': nrm((DEPTH, F2), 0.02),
        'ff_wdown': nrm((DEPTH, D_FF, D), D_FF ** -0.5),
    }


def reference(x_prompt, x_sample, state_shift, state_wkv, cache_kv_w128, cache_kv_w512,
              cache_kv_w2048, state_conv, norm_g, rw_mu, rw_wr, rw_wk, rw_wv, rw_wo,
              rw_w0, rw_w1, rw_w2, rw_a0, rw_a1, rw_a2, rw_g1, rw_g2, rw_kk, rw_ka, rw_rk,
              rw_lnx_g, rw_lnx_b, at_wqkv, at_wo, ff_wup, ff_conv_w, ff_conv_b, ff_wdown):
    rw = (rw_mu, rw_wr, rw_wk, rw_wv, rw_wo, rw_w0, rw_w1, rw_w2, rw_a0, rw_a1, rw_a2,
          rw_g1, rw_g2, rw_kk, rw_ka, rw_rk, rw_lnx_g, rw_lnx_b)
    ff = (ff_wup, ff_conv_w, ff_conv_b, ff_wdown)
    dt = x_prompt.dtype
    bp, tp = x_prompt.shape[0], x_prompt.shape[1]
    ts = x_sample.shape[1]
    pos_p = jnp.arange(tp, dtype=jnp.int32)
    y_prompt, p_shift, p_wkv, p_kv, p_conv = run_trunk(
        x_prompt, pos_p,
        jnp.zeros((N_A, bp, D_MODEL), dt),
        jnp.zeros((N_A, bp, RW_HEADS, RW_HEAD, RW_HEAD), dt),
        None,
        jnp.zeros((DEPTH, bp, CONV_W - 1, 2 * D_FF), dt),
        norm_g, rw, at_wqkv, at_wo, ff)
    pos_s = PAST_LEN + jnp.arange(ts, dtype=jnp.int32)
    y_sample, s_shift, s_wkv, s_kv, s_conv = run_trunk(
        x_sample, pos_s, state_shift, state_wkv,
        (cache_kv_w128, cache_kv_w512, cache_kv_w2048), state_conv,
        norm_g, rw, at_wqkv, at_wo, ff)
    return (y_prompt, y_sample, p_shift, p_wkv, p_kv[0], p_kv[1], p_kv[2], p_conv,
            s_shift, s_wkv, s_kv[0], s_kv[1], s_kv[2], s_conv)
```

```python
import functools
import math

import jax
import jax.numpy as jnp
from jax import lax
from jax.experimental import pallas as pl
from jax.experimental.pallas import tpu as pltpu

F32 = jnp.float32
BF16 = jnp.bfloat16

D_MODEL = 1024
RW_HEAD = 64
RW_HEADS = D_MODEL // RW_HEAD
RW_GN_EPS = 64e-5
ATT_GROUPS = ((128, 1), (512, 4), (2048, 16))
N_GROUPS = 3
ATT_HEADS = 8
ATT_HEAD_DIM = 64
ATT_WIDTH = ATT_HEADS * ATT_HEAD_DIM
ATT_SCALE = ATT_HEAD_DIM ** -0.5
ATT_BLK = 128
ROT_DIM = ATT_HEAD_DIM // 4
ROPE_THETA = 500000.0
D_FF = 2816
CONV_W = 3
NORM_EPS = 1e-6
PAST_LEN = 2048

LANES = 128
FF_CHUNK = 256
SCAN_CHUNK = 128
SCAN_HALF = 64
VMEM_LIMIT = 56 * 1024 * 1024


def _cparams(*sem):
    return pltpu.CompilerParams(dimension_semantics=sem, vmem_limit_bytes=VMEM_LIMIT)


def _dot(a, b):
    return jnp.dot(a, b, preferred_element_type=F32)


def _rms(x, g):
    ms = jnp.mean(x * x, axis=-1, keepdims=True)
    return x * lax.rsqrt(ms + NORM_EPS) * g


def _segsum(x, ones):
    outs = []
    for c in range(x.shape[1] // LANES):
        xc = x[:, c * LANES:(c + 1) * LANES]
        hi = xc.astype(BF16)
        lo = (xc - hi.astype(F32)).astype(BF16)
        outs.append(_dot(hi, ones) + _dot(lo, ones))
    return jnp.concatenate(outs, axis=1)


def _rms_kernel(x_ref, g_ref, o_ref):
    o_ref[...] = _rms(x_ref[...], g_ref[...])


def _rmsnorm_call(x, g, tm):
    n, d = x.shape
    return pl.pallas_call(
        _rms_kernel,
        grid=(n // tm,),
        in_specs=[pl.BlockSpec((tm, d), lambda i: (i, 0)),
                  pl.BlockSpec((1, d), lambda i: (0, 0))],
        out_specs=pl.BlockSpec((tm, d), lambda i: (i, 0)),
        out_shape=jax.ShapeDtypeStruct((n, d), F32),
        compiler_params=_cparams("parallel"),
        name="rmsnorm",
    )(x, g.reshape(1, d))


def _rwkv_proj_kernel(hn_ref, prev_ref, mu_ref, vec_ref, wr_ref, wk_ref, wv_ref, w1_ref, w2_ref,
                      a1_ref, a2_ref, g1_ref, g2_ref, ones_ref,
                      r_ref, d_ref, k_ref, v_ref, kk_ref, ka_ref, g_ref):
    hn = hn_ref[...]
    xx = prev_ref[...] - hn

    def mix(i):
        return (hn + xx * mu_ref[i:i + 1, :]).astype(BF16)

    w0 = vec_ref[0:1, :]
    a0 = vec_ref[1:2, :]
    k_k = vec_ref[2:3, :]
    k_a = vec_ref[3:4, :]

    r = _dot(mix(0), wr_ref[...])
    k = _dot(mix(2), wk_ref[...])
    v = _dot(mix(3), wv_ref[...])
    wl = _dot(jnp.tanh(_dot(mix(1), w1_ref[...])).astype(BF16), w2_ref[...])
    z = -(w0 + wl)
    softplus = jnp.maximum(z, 0.0) + jnp.log1p(jnp.exp(-jnp.abs(z)))
    w = -softplus - 0.5
    decay = jnp.exp(-jnp.exp(w))
    al = _dot(_dot(mix(4), a1_ref[...]).astype(BF16), a2_ref[...])
    a = jax.nn.sigmoid(a0 + al)
    g = _dot(jax.nn.sigmoid(_dot(mix(5), g1_ref[...])).astype(BF16), g2_ref[...])

    kk = k * k_k
    nrm = jnp.sqrt(_segsum(kk * kk, ones_ref[...]))
    kk = kk / jnp.maximum(nrm, 1e-12)
    k2 = k * (1.0 + (a - 1.0) * k_a)

    r_ref[...] = r
    d_ref[...] = decay
    k_ref[...] = k2
    v_ref[...] = v
    kk_ref[...] = kk
    ka_ref[...] = kk * a
    g_ref[...] = g


def _rwkv_proj_call(hn, prev, mu8, vec8, ws, ones, tm):
    n, d = hn.shape
    row = pl.BlockSpec((tm, d), lambda i: (i, 0))

    def full(a):
        return pl.BlockSpec(a.shape, lambda i: (0,) * a.ndim)

    return pl.pallas_call(
        _rwkv_proj_kernel,
        grid=(n // tm,),
        in_specs=[row, row, full(mu8), full(vec8)] + [full(w) for w in ws] + [full(ones)],
        out_specs=[row] * 7,
        out_shape=[jax.ShapeDtypeStruct((n, d), F32)] * 7,
        compiler_params=_cparams("parallel"),
        name="rwkv_proj",
    )(hn, prev, mu8, vec8, *ws, ones)


def _scan_kernel(kk_ref, d_ref, k_ref, ka_ref, r_ref, vt_ref, s0_ref, selv_ref, place_ref,
                 ones_ref, yt_ref, sout_ref, s_scr, lv_scr, *, steps):
    c = pl.program_id(1)

    @pl.when(c == 0)
    def _():
        s_scr[...] = s0_ref[0]

    vt4 = vt_ref[0].reshape(RW_HEADS // 2, 2, RW_HEAD, LANES)
    lv_scr[:, 0:LANES] = vt4[:, 0].reshape(8 * RW_HEAD, LANES).astype(BF16)
    lv_scr[:, LANES:2 * LANES] = vt4[:, 1].reshape(8 * RW_HEAD, LANES).astype(BF16)
    ones = ones_ref[...]

    def rows(blk, j):
        return jnp.concatenate(
            [jnp.broadcast_to(blk[j:j + 1, hp * LANES:(hp + 1) * LANES], (RW_HEAD, LANES))
             for hp in range(RW_HEADS // 2)], axis=0)

    n_half = -(-steps // SCAN_HALF)
    for half in range(n_half):
        nst = min(SCAN_HALF, steps - half * SCAN_HALF)
        lo = half * LANES
        yt_ref[0, :, lo:lo + LANES] = jnp.zeros((8 * RW_HEAD, LANES), F32)

        def body(grp, carry, half=half, lo=lo):
            base = pl.multiple_of(half * SCAN_HALF + grp * 8, 8)
            kk8, d8, k8, ka8, r8 = [ref[0, pl.ds(base, 8), :]
                                    for ref in (kk_ref, d_ref, k_ref, ka_ref, r_ref)]
            for j in range(8):
                s = s_scr[...]
                sk = _dot((s * rows(kk8, j)).astype(BF16), ones)
                vcol = _dot(lv_scr[...], selv_ref[base + j])
                s = s * rows(d8, j) - sk * rows(ka8, j) + vcol * rows(k8, j)
                s_scr[...] = s
                p2 = (s * rows(r8, j)).astype(BF16)
                yt_ref[0, :, lo:lo + LANES] += _dot(p2, place_ref[grp * 8 + j])
            return carry

        lax.fori_loop(0, nst // 8, body, 0)

    @pl.when(c == pl.num_programs(1) - 1)
    def _():
        sout_ref[0] = s_scr[...]


def _scan_tables():
    t = jnp.arange(SCAN_CHUNK)[:, None, None]
    row = jnp.arange(2 * LANES)[None, :, None]
    col = jnp.arange(LANES)[None, None, :]
    selv = ((row // LANES == col // RW_HEAD) & (row % LANES == t)).astype(BF16)
    t2 = jnp.arange(SCAN_HALF)[:, None, None]
    row2 = jnp.arange(LANES)[None, :, None]
    place = ((row2 // RW_HEAD == col // RW_HEAD) & (col % RW_HEAD == t2)).astype(BF16)
    return selv, place


def _seg_ones():
    i = jnp.arange(LANES)
    return (i[:, None] // RW_HEAD == i[None, :] // RW_HEAD).astype(BF16)


def _scan_call(kk, dec, k, ka, r, vt, s0, steps):
    b, t, d = kk.shape
    nc = -(-t // SCAN_CHUNK)
    n_half = -(-steps // SCAN_HALF)
    selv, place = _scan_tables()
    ones = _seg_ones()
    rowspec = pl.BlockSpec((1, steps, d), lambda i, c: (i, c, 0))

    def full(a):
        return pl.BlockSpec(a.shape, lambda i, c: (0,) * a.ndim)

    sspec = pl.BlockSpec((1, 8 * RW_HEAD, LANES), lambda i, c: (i, 0, 0))
    yt, sout = pl.pallas_call(
        functools.partial(_scan_kernel, steps=steps),
        grid=(b, nc),
        in_specs=[rowspec] * 5 + [pl.BlockSpec((1, d, LANES), lambda i, c: (i, 0, c)), sspec,
                                  full(selv), full(place), full(ones)],
        out_specs=[pl.BlockSpec((1, 8 * RW_HEAD, n_half * LANES), lambda i, c: (i, 0, c)), sspec],
        out_shape=[jax.ShapeDtypeStruct((b, 8 * RW_HEAD, nc * n_half * LANES), F32),
                   jax.ShapeDtypeStruct((b, 8 * RW_HEAD, LANES), F32)],
        scratch_shapes=[pltpu.VMEM((8 * RW_HEAD, LANES), F32),
                        pltpu.VMEM((8 * RW_HEAD, 2 * LANES), BF16)],
        compiler_params=_cparams("parallel", "arbitrary"),
        name="rwkv_scan",
    )(kk, dec, k, ka, r, vt, s0, selv, place, ones)
    y = yt.reshape(b, 8, RW_HEAD, nc * n_half, 2, SCAN_HALF)
    y = y.transpose(0, 3, 5, 1, 4, 2).reshape(b, nc * n_half * SCAN_HALF, d)
    if n_half * SCAN_HALF != steps:
        y = y.reshape(b, nc, n_half * SCAN_HALF, d)[:, :, :steps].reshape(b, nc * steps, d)
    return y[:, :t], sout


def _state_to_scan(s):
    b = s.shape[0]
    return s.reshape(b, 8, 2, RW_HEAD, RW_HEAD).transpose(0, 1, 3, 2, 4).reshape(b, 8 * RW_HEAD, LANES)


def _state_from_scan(s):
    b = s.shape[0]
    return s.reshape(b, 8, RW_HEAD, 2, RW_HEAD).transpose(0, 1, 3, 2, 4).reshape(
        b, RW_HEADS, RW_HEAD, RW_HEAD)


def _rwkv_out_kernel(y_ref, r_ref, k_ref, v_ref, g_ref, x_ref, vec_ref, wo_ref, ones_ref, o_ref):
    ones = ones_ref[...]
    y = y_ref[...]
    mean = _segsum(y, ones) * (1.0 / RW_HEAD)
    yc = y - mean
    var = _segsum(yc * yc, ones) * (1.0 / RW_HEAD)
    yn = yc * lax.rsqrt(var + RW_GN_EPS) * vec_ref[0:1, :] + vec_ref[1:2, :]
    r = r_ref[...]
    bonus = _segsum(r * k_ref[...] * vec_ref[2:3, :], ones) * v_ref[...]
    z = ((yn + bonus) * g_ref[...]).astype(BF16)
    mo = _dot(z, wo_ref[...])
    o_ref[...] = x_ref[...] + _rms(mo, vec_ref[3:4, :])


def _rwkv_out_call(y, r, k, v, g, x, vec8, wo, ones, tm):
    n, d = y.shape
    row = pl.BlockSpec((tm, d), lambda i: (i, 0))

    def full(a):
        return pl.BlockSpec(a.shape, lambda i: (0,) * a.ndim)

    return pl.pallas_call(
        _rwkv_out_kernel,
        grid=(n // tm,),
        in_specs=[row] * 6 + [full(vec8), full(wo), full(ones)],
        out_specs=row,
        out_shape=jax.ShapeDtypeStruct((n, d), F32),
        compiler_params=_cparams("parallel"),
        name="rwkv_out",
    )(y, r, k, v, g, x, vec8, wo, ones)


def _gelu_tanh(x):
    return 0.5 * x * (1.0 + jnp.tanh(math.sqrt(2.0 / math.pi) * (x + 0.044715 * (x * x * x))))


def _ffn_kernel(*refs, seq_len, tm):
    if seq_len is None:
        x_ref, g_ref, wup_ref, cw_ref, cb_ref, wdown_ref, xo_ref, hl_ref, carry = refs
    else:
        x_ref, g_ref, wup_ref, cw_ref, cb_ref, wdown_ref, p1_ref, p2_ref, xo_ref, hl_ref = refs
    x = x_ref[...]
    hn = _rms(x, g_ref[0:1, :]).astype(BF16)
    rows = lax.broadcasted_iota(jnp.int32, (tm, FF_CHUNK), 0)
    if seq_len is None:
        @pl.when(pl.program_id(1) == 0)
        def _():
            carry[...] = jnp.zeros(carry.shape, F32)
    else:
        tpos = rows % seq_len

    acc = jnp.zeros((tm, D_MODEL), F32)
    for j in range(D_FF // FF_CHUNK):
        conv = []
        for part in range(2):
            c0 = part * D_FF + j * FF_CHUNK
            cs = slice(c0, c0 + FF_CHUNK)
            h = _dot(hn, wup_ref[:, cs])
            h1 = pltpu.roll(h, 1, axis=0)
            h2 = pltpu.roll(h, 2, axis=0)
            if seq_len is None:
                pm2 = carry[0:1, cs]
                pm1 = carry[1:2, cs]
                h1 = jnp.where(rows == 0, pm1, h1)
                h2 = jnp.where(rows == 0, pm2, jnp.where(rows == 1, pm1, h2))
                carry[0:2, cs] = h[tm - 2:tm, :]
                hl_ref[:, cs] = h[tm - 8:tm, :]
            else:
                h1 = jnp.where(tpos == 0, p1_ref[:, cs], h1)
                h2 = jnp.where(tpos < 2, p2_ref[:, cs], h2)
                hl_ref[:, cs] = h
            conv.append(cb_ref[0:1, cs] + h2 * cw_ref[0:1, cs] + h1 * cw_ref[1:2, cs]
                        + h * cw_ref[2:3, cs])
        act = (_gelu_tanh(conv[0]) * conv[1]).astype(BF16)
        acc = acc + _dot(act, wdown_ref[j * FF_CHUNK:(j + 1) * FF_CHUNK, :])
    xo_ref[...] = x + _rms(acc, g_ref[1:2, :])


def _ffn_call(x, g2, wup, cw8, cb8, wdown, tm, seq_len=None, prev=None, batch=None):
    n, d = x.shape
    f2 = 2 * D_FF
    kern = functools.partial(_ffn_kernel, seq_len=seq_len, tm=tm)
    if seq_len is None:
        tpb = n // batch // tm
        row = pl.BlockSpec((tm, d), lambda b, j: (b * tpb + j, 0))

        def full(a):
            return pl.BlockSpec(a.shape, lambda b, j: (0,) * a.ndim)

        return pl.pallas_call(
            kern,
            grid=(batch, tpb),
            in_specs=[row, full(g2), full(wup), full(cw8), full(cb8), full(wdown)],
            out_specs=[row, pl.BlockSpec((8, f2), lambda b, j: (b * tpb + j, 0))],
            out_shape=[jax.ShapeDtypeStruct((n, d), F32),
                       jax.ShapeDtypeStruct((n // tm * 8, f2), F32)],
            scratch_shapes=[pltpu.VMEM((8, f2), F32)],
            compiler_params=_cparams("parallel", "arbitrary"),
            name="conv_ffn_long",
        )(x, g2, wup, cw8, cb8, wdown)
    p1, p2 = prev
    row = pl.BlockSpec((tm, d), lambda i: (i, 0))
    hrow = pl.BlockSpec((tm, f2), lambda i: (i, 0))

    def full(a):
        return pl.BlockSpec(a.shape, lambda i: (0,) * a.ndim)

    return pl.pallas_call(
        kern,
        grid=(n // tm,),
        in_specs=[row, full(g2), full(wup), full(cw8), full(cb8), full(wdown), hrow, hrow],
        out_specs=[row, hrow],
        out_shape=[jax.ShapeDtypeStruct((n, d), F32), jax.ShapeDtypeStruct((n, f2), F32)],
        compiler_params=_cparams("parallel"),
        name="conv_ffn_short",
    )(x, g2, wup, cw8, cb8, wdown, p1, p2)


def _qkv_kernel(x_ref, g_ref, w_ref, ct_ref, s1_ref, s2_ref, o_ref):
    hn = _rms(x_ref[...], g_ref[...]).astype(BF16)
    ct = ct_ref[...]
    s1 = s1_ref[...]
    s2 = s2_ref[...]
    nrot = 2 * N_GROUPS * ATT_WIDTH // LANES
    for c in range(3 * N_GROUPS * ATT_WIDTH // LANES):
        cs = slice(c * LANES, (c + 1) * LANES)
        xb = _dot(hn, w_ref[:, cs])
        if c < nrot:
            xb = (xb * ct + pltpu.roll(xb, LANES - ROT_DIM // 2, axis=1) * s1
                  + pltpu.roll(xb, ROT_DIM // 2, axis=1) * s2)
        o_ref[:, cs] = xb


def _rope_tables(pos):
    half = ROT_DIM // 2
    inv = jnp.power(ROPE_THETA, -jnp.arange(half, dtype=F32) / half)
    ang = pos.astype(F32)[:, None] * inv[None, :]
    cos, sin = jnp.cos(ang), jnp.sin(ang)
    n = pos.shape[0]
    one = jnp.ones((n, ATT_HEAD_DIM - ROT_DIM), F32)
    zero = jnp.zeros((n, ATT_HEAD_DIM - ROT_DIM), F32)
    zh = jnp.zeros((n, half), F32)
    ct = jnp.concatenate([cos, cos, one], axis=1)
    s1 = jnp.concatenate([-sin, zh, zero], axis=1)
    s2 = jnp.concatenate([zh, sin, zero], axis=1)
    rep = LANES // ATT_HEAD_DIM
    return tuple(jnp.tile(a, (1, rep)) for a in (ct, s1, s2))


def _qkv_call(x, g, wqkv, tables, tm, tab_blocks):
    n, d = x.shape
    nq = wqkv.shape[1]
    row = pl.BlockSpec((tm, d), lambda i: (i, 0))
    tab = pl.BlockSpec((tm, LANES), lambda i: (i % tab_blocks, 0))
    return pl.pallas_call(
        _qkv_kernel,
        grid=(n // tm,),
        in_specs=[row, pl.BlockSpec((1, d), lambda i: (0, 0)),
                  pl.BlockSpec(wqkv.shape, lambda i: (0, 0)), tab, tab, tab],
        out_specs=pl.BlockSpec((tm, nq), lambda i: (i, 0)),
        out_shape=jax.ShapeDtypeStruct((n, nq), F32),
        compiler_params=_cparams("parallel"),
        name="attn_qkv_rope",
    )(x, g.reshape(1, d), wqkv, *tables)


def _attn_prompt_kernel(q_ref, kp_ref, kc_ref, vp_ref, vc_ref, o_ref, l_ref):
    n = pl.program_id(2)
    blk = ATT_BLK
    qi = lax.broadcasted_iota(jnp.int32, (blk, 2 * blk), 0)
    kj = lax.broadcasted_iota(jnp.int32, (blk, 2 * blk), 1)
    dist = qi + blk - kj
    valid = (dist >= 0) & (dist <= blk) & (n * blk + kj - blk >= 0)
    q = q_ref[0] * ATT_SCALE
    for h in range(ATT_HEADS):
        hs = slice(h * ATT_HEAD_DIM, (h + 1) * ATT_HEAD_DIM)
        qh = q[:, hs].astype(BF16)
        kh = jnp.concatenate([kp_ref[0, :, hs], kc_ref[0, :, hs]], axis=0).astype(BF16)
        vh = jnp.concatenate([vp_ref[0, :, hs], vc_ref[0, :, hs]], axis=0).astype(BF16)
        s = lax.dot_general(qh, kh, (((1,), (1,)), ((), ())), preferred_element_type=F32)
        s = jnp.where(valid, s, -jnp.inf)
        m = jnp.max(s, axis=-1, keepdims=True)
        p = jnp.exp(s - m)
        l = jnp.sum(p, axis=-1, keepdims=True)
        o = _dot(p.astype(BF16), vh) / l
        o_ref[0, :, hs] = o
        l_ref[0, :, hs] = jnp.broadcast_to(m + jnp.log(l), (blk, ATT_HEAD_DIM))


def _attn_prompt_call(qkv, gi, b, s):
    window, dil = ATT_GROUPS[gi]
    blk = window // dil
    n_sub = s // dil
    nb = n_sub // blk
    ncol = 3 * N_GROUPS
    view = qkv.reshape(b, n_sub, dil * ncol * ATT_WIDTH)
    blkshape = (1, blk, ATT_WIDTH)

    def cur(part):
        return pl.BlockSpec(blkshape, lambda i, r, n: (i, n, r * ncol + part * N_GROUPS + gi))

    def prv(part):
        return pl.BlockSpec(blkshape,
                            lambda i, r, n: (i, jnp.maximum(n - 1, 0), r * ncol + part * N_GROUPS + gi))

    ospec = pl.BlockSpec(blkshape, lambda i, r, n: (i, n, r))
    o, lse = pl.pallas_call(
        _attn_prompt_kernel,
        grid=(b, dil, nb),
        in_specs=[cur(0), prv(1), cur(1), prv(2), cur(2)],
        out_specs=[ospec, ospec],
        out_shape=[jax.ShapeDtypeStruct((b, n_sub, dil * ATT_WIDTH), F32)] * 2,
        compiler_params=_cparams("parallel", "parallel", "parallel"),
        name=f"attn_prompt_g{gi}",
    )(view, view, view, view, view)
    return o.reshape(b * s, ATT_WIDTH), lse.reshape(b * s, ATT_WIDTH)


def _attn_sample_kernel(qkv_ref, c0_ref, c1_ref, c2_ref, o_ref, *, t):
    caches = (c0_ref, c1_ref, c2_ref)
    nrow = ATT_HEADS * t
    rowi = lax.broadcasted_iota(jnp.int32, (nrow, ATT_WIDTH), 0)
    coli = lax.broadcasted_iota(jnp.int32, (nrow, ATT_WIDTH), 1)
    headmask = (rowi // t) == (coli // ATT_HEAD_DIM)
    pad = jnp.zeros((LANES - t, ATT_WIDTH), F32)
    nt = (((1,), (1,)), ((), ()))

    stats = []
    for gi in range(N_GROUPS):
        window, dil = ATT_GROUPS[gi]
        cref = caches[gi]
        ln = cref.shape[1]
        q = qkv_ref[0, :, gi * ATT_WIDTH:(gi + 1) * ATT_WIDTH] * ATT_SCALE
        knew = qkv_ref[0, :, (N_GROUPS + gi) * ATT_WIDTH:(N_GROUPS + gi + 1) * ATT_WIDTH]
        vnew = qkv_ref[0, :, (2 * N_GROUPS + gi) * ATT_WIDTH:(2 * N_GROUPS + gi + 1) * ATT_WIDTH]
        qexp = jnp.where(headmask, jnp.concatenate([q] * ATT_HEADS, axis=0), 0.0).astype(BF16)
        kc = cref[0, :, 0:ATT_WIDTH].astype(BF16)
        vc = cref[0, :, ATT_WIDTH:2 * ATT_WIDTH].astype(BF16)
        kn = jnp.concatenate([knew, pad], axis=0).astype(BF16)
        vn = jnp.concatenate([vnew, pad], axis=0).astype(BF16)
        sc = lax.dot_general(qexp, kc, nt, preferred_element_type=F32)
        sn = lax.dot_general(qexp, kn, nt, preferred_element_type=F32)
        qi_c = lax.broadcasted_iota(jnp.int32, (nrow, ln), 0) % t
        pos_c = lax.broadcasted_iota(jnp.int32, (nrow, ln), 1)
        delta_c = ln + qi_c - pos_c
        ok_c = (delta_c <= window) & (delta_c % dil == 0)
        qi_n = lax.broadcasted_iota(jnp.int32, (nrow, LANES), 0) % t
        j_n = lax.broadcasted_iota(jnp.int32, (nrow, LANES), 1)
        delta_n = qi_n - j_n
        ok_n = (j_n < t) & (delta_n >= 0) & (delta_n <= window) & (delta_n % dil == 0)
        sc = jnp.where(ok_c, sc, -jnp.inf)
        sn = jnp.where(ok_n, sn, -jnp.inf)
        m = jnp.maximum(jnp.max(sc, axis=-1, keepdims=True), jnp.max(sn, axis=-1, keepdims=True))
        pc = jnp.exp(sc - m)
        pn = jnp.exp(sn - m)
        l = jnp.sum(pc, axis=-1, keepdims=True) + jnp.sum(pn, axis=-1, keepdims=True)
        og = _dot(pc.astype(BF16), vc) + _dot(pn.astype(BF16), vn)
        stats.append((m + jnp.log(l), l, og))

    mx = functools.reduce(jnp.maximum, [s[0] for s in stats])
    es = [jnp.exp(s[0] - mx) for s in stats]
    den = es[0] + es[1] + es[2]
    full = sum(s[2] * (e / (den * s[1])) for s, e in zip(stats, es))
    for h in range(ATT_HEADS):
        hs = slice(h * ATT_HEAD_DIM, (h + 1) * ATT_HEAD_DIM)
        o_ref[0, :, hs] = full[h * t:(h + 1) * t, hs]


def _attn_sample_call(qkv, caches, b, t):
    nq = qkv.shape[-1]
    views = [c.reshape(b, c.shape[-4], 2 * ATT_WIDTH) for c in caches]
    return pl.pallas_call(
        functools.partial(_attn_sample_kernel, t=t),
        grid=(b,),
        in_specs=[pl.BlockSpec((1, t, nq), lambda i: (i, 0, 0))]
                 + [pl.BlockSpec((1,) + v.shape[1:], lambda i: (i, 0, 0)) for v in views],
        out_specs=pl.BlockSpec((1, t, ATT_WIDTH), lambda i: (i, 0, 0)),
        out_shape=jax.ShapeDtypeStruct((b, t, ATT_WIDTH), F32),
        compiler_params=_cparams("parallel"),
        name="attn_sample",
    )(qkv.reshape(b, t, nq), *views)


def _attn_out_kernel(*refs, merge):
    if merge:
        o0, o1, o2, l0, l1, l2, x_ref, g_ref, wo_ref, out_ref = refs
        la, lb, lc = l0[...], l1[...], l2[...]
        mx = jnp.maximum(jnp.maximum(la, lb), lc)
        ea, eb, ec = jnp.exp(la - mx), jnp.exp(lb - mx), jnp.exp(lc - mx)
        den = ea + eb + ec
        o = (ea / den) * o0[...] + (eb / den) * o1[...] + (ec / den) * o2[...]
    else:
        o0, x_ref, g_ref, wo_ref, out_ref = refs
        o = o0[...]
    y = _dot(o.astype(BF16), wo_ref[...])
    out_ref[...] = x_ref[...] + _rms(y, g_ref[...])


def _attn_out_call(os_, lses, x, g, wo, tm):
    n, d = x.shape
    row = pl.BlockSpec((tm, d), lambda i: (i, 0))
    arow = pl.BlockSpec((tm, ATT_WIDTH), lambda i: (i, 0))
    merge = lses is not None
    ins = list(os_) + (list(lses) if merge else [])
    return pl.pallas_call(
        functools.partial(_attn_out_kernel, merge=merge),
        grid=(n // tm,),
        in_specs=[arow] * len(ins) + [row, pl.BlockSpec((1, d), lambda i: (0, 0)),
                                      pl.BlockSpec(wo.shape, lambda i: (0, 0))],
        out_specs=row,
        out_shape=jax.ShapeDtypeStruct((n, d), F32),
        compiler_params=_cparams("parallel"),
        name="attn_out",
    )(*ins, x, g.reshape(1, d), wo)


def _pad_rows8(a):
    return jnp.concatenate([a, jnp.zeros((8 - a.shape[0],) + a.shape[1:], a.dtype)], axis=0)


def _pad_to(a, axis, size):
    padw = [(0, 0)] * a.ndim
    padw[axis] = (0, size - a.shape[axis])
    return jnp.pad(a, padw)


def _prep_weights(norm_g, rw, at_wqkv, at_wo, ff):
    (mu, wr, wk, wv, wo, w0, w1, w2, a0, a1, a2, g1, g2, k_k, k_a, r_k, lnx_g, lnx_b) = [p[0] for p in rw]
    glora = 2 * LANES
    proj_ws = [wr.astype(BF16), wk.astype(BF16), wv.astype(BF16),
               _pad_to(w1, 1, LANES).astype(BF16), _pad_to(w2, 0, LANES).astype(BF16),
               _pad_to(a1, 1, LANES).astype(BF16), _pad_to(a2, 0, LANES).astype(BF16),
               _pad_to(g1, 1, glora).astype(BF16), _pad_to(g2, 0, glora).astype(BF16)]
    return dict(
        mu8=_pad_rows8(mu),
        proj_vec=_pad_rows8(jnp.stack([w0, a0, k_k, k_a])),
        proj_ws=proj_ws,
        out_vec=_pad_rows8(jnp.stack([lnx_g, lnx_b, r_k.reshape(-1), norm_g[0, 1]])),
        rw_wo=wo.astype(BF16),
        wqkv=at_wqkv[0].astype(BF16),
        at_wo=at_wo[0].astype(BF16),
        ff=[(_pad_rows8(jnp.stack([norm_g[l, 2], norm_g[l, 3]])), ff[0][l].astype(BF16),
             _pad_rows8(ff[1][l]), _pad_rows8(ff[2][l][None]), ff[3][l].astype(BF16))
            for l in range(2)],
        ones=_seg_ones(),
    )


def _run_group(x, pos, shift0, wkv0, kv_bufs, conv0, norm_g, w, tm):
    b, t, d = x.shape
    n = b * t
    tm = min(tm, n)
    long_mode = shift0 is None
    xf = x.reshape(n, d)

    hn = _rmsnorm_call(xf, norm_g[0, 0], tm).reshape(b, t, d)
    first = jnp.zeros((b, 1, d), F32) if long_mode else shift0[0][:, None, :]
    prev = jnp.concatenate([first, hn[:, :-1]], axis=1).reshape(n, d)
    r, dec, k, v, kk, ka, g = _rwkv_proj_call(hn.reshape(n, d), prev, w["mu8"], w["proj_vec"],
                                              w["proj_ws"], w["ones"], min(tm, 256))
    steps = min(t, SCAN_CHUNK)
    vt = v.reshape(b, t, d).transpose(0, 2, 1)
    if t < LANES:
        vt = _pad_to(vt, 2, LANES)
    s0 = jnp.zeros((b, 8 * RW_HEAD, LANES), F32) if long_mode else _state_to_scan(wkv0[0])
    to3 = lambda a: a.reshape(b, t, d)
    y, s_fin = _scan_call(to3(kk), to3(dec), to3(k), to3(ka), to3(r), vt, s0, steps)
    x1 = _rwkv_out_call(y.reshape(n, d), r, k, v, g, xf, w["out_vec"], w["rw_wo"], w["ones"], tm)
    new_shift = hn[:, -1][None]
    new_wkv = _state_from_scan(s_fin)[None]

    def ffn(xin, layer):
        g2, wup, cw8, cb8, wdown = w["ff"][layer]
        if long_mode:
            xo, hl = _ffn_call(xin, g2, wup, cw8, cb8, wdown, tm, batch=b)
            cs = hl.reshape(b, t // tm, 8, 2 * D_FF)[:, -1, 8 - (CONV_W - 1):]
        else:
            st = conv0[layer]
            z = jnp.zeros((b, t - 1, 2 * D_FF), F32)
            p1 = jnp.concatenate([st[:, 1:2], z], axis=1).reshape(n, 2 * D_FF)
            p2 = jnp.concatenate([st, z[:, 1:]], axis=1).reshape(n, 2 * D_FF)
            xo, hl = _ffn_call(xin, g2, wup, cw8, cb8, wdown, tm, seq_len=t, prev=(p1, p2))
            cs = hl.reshape(b, t, 2 * D_FF)[:, t - (CONV_W - 1):]
        return xo, cs

    x2, conv_a = ffn(x1, 0)

    if long_mode:
        tables = _rope_tables(pos)
        tab_blocks = t // tm
    else:
        tables = tuple(jnp.tile(a, (tm // t, 1)) for a in _rope_tables(pos))
        tab_blocks = 1
    qkv = _qkv_call(x2, norm_g[1, 0], w["wqkv"], tables, tm, tab_blocks)
    qkv3 = qkv.reshape(b, t, 3 * N_GROUPS, ATT_HEADS, ATT_HEAD_DIM)
    new_kv = []
    for gi in range(N_GROUPS):
        keep = min(ATT_GROUPS[gi][0], t) if long_mode else t
        kg = qkv3[:, t - keep:, N_GROUPS + gi]
        vg = qkv3[:, t - keep:, 2 * N_GROUPS + gi]
        new_kv.append(jnp.stack([kg, vg], axis=2)[None])
    if long_mode:
        res = [_attn_prompt_call(qkv.reshape(b, t, -1), gi, b, t) for gi in range(N_GROUPS)]
        x3 = _attn_out_call([o for o, _ in res], [l for _, l in res], x2, norm_g[1, 1], w["at_wo"], tm)
    else:
        o = _attn_sample_call(qkv, [buf[0] for buf in kv_bufs], b, t)
        x3 = _attn_out_call([o.reshape(n, ATT_WIDTH)], None, x2, norm_g[1, 1], w["at_wo"], tm)
    x4, conv_b = ffn(x3, 1)
    return (x4.reshape(b, t, d), new_shift, new_wkv, new_kv, jnp.stack([conv_a, conv_b]))


def kernel(x_prompt, x_sample, state_shift, state_wkv, cache_kv_w128, cache_kv_w512, cache_kv_w2048, state_conv, norm_g, rw_mu, rw_wr, rw_wk, rw_wv, rw_wo, rw_w0, rw_w1, rw_w2, rw_a0, rw_a1, rw_a2, rw_g1, rw_g2, rw_kk, rw_ka, rw_rk, rw_lnx_g, rw_lnx_b, at_wqkv, at_wo, ff_wup, ff_conv_w, ff_conv_b, ff_wdown):
    rw = (rw_mu, rw_wr, rw_wk, rw_wv, rw_wo, rw_w0, rw_w1, rw_w2, rw_a0, rw_a1, rw_a2,
          rw_g1, rw_g2, rw_kk, rw_ka, rw_rk, rw_lnx_g, rw_lnx_b)
    ff = (ff_wup, ff_conv_w, ff_conv_b, ff_wdown)
    w = _prep_weights(norm_g, rw, at_wqkv, at_wo, ff)
    tp = x_prompt.shape[1]
    ts = x_sample.shape[1]
    pos_p = jnp.arange(tp, dtype=jnp.int32)
    y_p, p_shift, p_wkv, p_kv, p_conv = _run_group(
        x_prompt, pos_p, None, None, None, None, norm_g, w, 512)
    pos_s = PAST_LEN + jnp.arange(ts, dtype=jnp.int32)
    y_s, s_shift, s_wkv, s_kv, s_conv = _run_group(
        x_sample, pos_s, state_shift, state_wkv,
        (cache_kv_w128, cache_kv_w512, cache_kv_w2048), state_conv, norm_g, w, 256)
    return (y_p, y_s, p_shift, p_wkv, p_kv[0], p_kv[1], p_kv[2], p_conv,
            s_shift, s_wkv, s_kv[0], s_kv[1], s_kv[2], s_conv)
```

```python
import functools
import math

import jax
import jax.numpy as jnp
from jax import lax
from jax.experimental import pallas as pl
from jax.experimental.pallas import tpu as pltpu

F32 = jnp.float32
BF16 = jnp.bfloat16

D_MODEL = 1024
RW_HEAD = 64
RW_HEADS = D_MODEL // RW_HEAD
RW_GN_EPS = 64e-5
ATT_GROUPS = ((128, 1), (512, 4), (2048, 16))
N_GROUPS = 3
ATT_HEADS = 8
ATT_HEAD_DIM = 64
ATT_WIDTH = ATT_HEADS * ATT_HEAD_DIM
ATT_SCALE = ATT_HEAD_DIM ** -0.5
ATT_BLK = 128
ROT_DIM = ATT_HEAD_DIM // 4
ROPE_THETA = 500000.0
D_FF = 2816
CONV_W = 3
NORM_EPS = 1e-6
PAST_LEN = 2048

LANES = 128
SUBLANES = 8
FF_CHUNK = 256
SCAN_CHUNK = 128
SCAN_HALF = 64
SCAN_ROWS = RW_HEADS // 2 * RW_HEAD
SCAN_NB = 2
VMEM_LIMIT = 56 * 1024 * 1024

SCAN_HEAD_ORDER = tuple(4 * (q // 4) + (0, 2, 1, 3)[q % 4] for q in range(RW_HEADS))


def _cparams(*sem):
    return pltpu.CompilerParams(dimension_semantics=sem, vmem_limit_bytes=VMEM_LIMIT)


def _dot(a, b):
    return jnp.dot(a, b, preferred_element_type=F32)


def _rms(x, g):
    ms = jnp.mean(x * x, axis=-1, keepdims=True)
    return x * lax.rsqrt(ms + NORM_EPS) * g


def _segsum(x, ones):
    outs = []
    for c in range(x.shape[1] // LANES):
        xc = x[:, c * LANES:(c + 1) * LANES]
        hi = xc.astype(BF16)
        lo = (xc - hi.astype(F32)).astype(BF16)
        outs.append(_dot(hi, ones) + _dot(lo, ones))
    return jnp.concatenate(outs, axis=1)


def _scan_to_natural_heads(x):
    low = lax.broadcasted_iota(jnp.int32, (x.shape[0], LANES), 1) < RW_HEAD
    outs = []
    for c in range(0, x.shape[1] // LANES, 2):
        a = x[:, c * LANES:(c + 1) * LANES]
        b = x[:, (c + 1) * LANES:(c + 2) * LANES]
        outs.append(jnp.where(low, a, pltpu.roll(b, RW_HEAD, axis=1)))
        outs.append(jnp.where(low, pltpu.roll(a, RW_HEAD, axis=1), b))
    return jnp.concatenate(outs, axis=1)


def _rwkv_proj_kernel(*refs, seq_len, tm):
    if seq_len is None:
        (x_ref, gn_ref, mu_ref, vec_ref, wr_ref, wk_ref, wv_ref, wvt_ref, w1_ref, w2_ref, a1_ref,
         a2_ref, g1_ref, g2_ref, ones_ref,
         r_ref, d_ref, k_ref, kk_ref, ka_ref, g_ref, bonus_ref, vt_ref, hl_ref, carry) = refs
    else:
        (x_ref, gn_ref, mu_ref, vec_ref, wr_ref, wk_ref, wv_ref, w1_ref, w2_ref, a1_ref,
         a2_ref, g1_ref, g2_ref, ones_ref, sh_ref,
         r_ref, d_ref, k_ref, kk_ref, ka_ref, g_ref, bonus_ref, v_ref, hn_ref) = refs
    hn = _rms(x_ref[...], gn_ref[...])
    rolled = pltpu.roll(hn, 1, axis=0)
    rows = lax.broadcasted_iota(jnp.int32, (tm, D_MODEL), 0)
    if seq_len is None:
        @pl.when(pl.program_id(1) == 0)
        def _():
            carry[...] = jnp.zeros(carry.shape, F32)

        prev = jnp.where(rows == 0, carry[0:1, :], rolled)
        carry[0:1, :] = hn[tm - 1:tm, :]
        hl_ref[0] = hn[tm - SUBLANES:tm, :]
    else:
        prev = jnp.where(rows % seq_len == 0, sh_ref[...], rolled)
        hn_ref[...] = hn
    xx = prev - hn

    def mix(i):
        return (hn + xx * mu_ref[i:i + 1, :]).astype(BF16)

    w0 = vec_ref[0:1, :]
    a0 = vec_ref[1:2, :]
    k_k = vec_ref[2:3, :]
    k_a = vec_ref[3:4, :]
    r_k = vec_ref[4:5, :]

    r = _dot(mix(0), wr_ref[...])
    k = _dot(mix(2), wk_ref[...])
    xv = mix(3)
    v = _dot(xv, wv_ref[...])
    wl = _dot(jnp.tanh(_dot(mix(1), w1_ref[...])).astype(BF16), w2_ref[...])
    z = -(w0 + wl)
    softplus = jnp.maximum(z, 0.0) + jnp.log1p(jnp.exp(-jnp.abs(z)))
    w = -softplus - 0.5
    decay = jnp.exp(-jnp.exp(w))
    al = _dot(_dot(mix(4), a1_ref[...]).astype(BF16), a2_ref[...])
    a = jax.nn.sigmoid(a0 + al)
    g = _dot(jax.nn.sigmoid(_dot(mix(5), g1_ref[...])).astype(BF16), g2_ref[...])

    ones = ones_ref[...]
    kk = k * k_k
    nrm = jnp.sqrt(_segsum(kk * kk, ones))
    kk = kk / jnp.maximum(nrm, 1e-12)
    k2 = k * (1.0 + (a - 1.0) * k_a)

    r_ref[...] = r
    d_ref[...] = decay
    k_ref[...] = k2
    kk_ref[...] = kk
    ka_ref[...] = kk * a
    g_ref[...] = g
    bonus_ref[...] = _scan_to_natural_heads(_segsum(r * k2 * r_k, ones) * v)
    if seq_len is None:
        vt_ref[0] = lax.dot_general(wvt_ref[...], xv, (((1,), (1,)), ((), ())),
                                    preferred_element_type=F32)
    else:
        v_ref[...] = v


def _rwkv_proj_call(x, gn, w, tm, batch=None, seq_len=None, shift_rows=None):
    n, d = x.shape
    nd = [jax.ShapeDtypeStruct((n, d), F32)]
    ws = w["proj_ws"]
    if seq_len is None:
        t = n // batch
        tpb = t // tm
        row = pl.BlockSpec((tm, d), lambda b, j: (b * tpb + j, 0))

        def full(a):
            return pl.BlockSpec(a.shape, lambda b, j: (0,) * a.ndim)

        ins = [x, gn, w["mu8"], w["proj_vec"], ws[0], ws[1], ws[2], w["wvt"]] + ws[3:] + [w["ones"]]
        return pl.pallas_call(
            functools.partial(_rwkv_proj_kernel, seq_len=None, tm=tm),
            grid=(batch, tpb),
            in_specs=[row] + [full(a) for a in ins[1:]],
            out_specs=[row] * 7 + [pl.BlockSpec((1, d, tm), lambda b, j: (b, 0, j)),
                                   pl.BlockSpec((1, SUBLANES, d), lambda b, j: (b, 0, 0))],
            out_shape=nd * 7 + [jax.ShapeDtypeStruct((batch, d, t), F32),
                                jax.ShapeDtypeStruct((batch, SUBLANES, d), F32)],
            scratch_shapes=[pltpu.VMEM((SUBLANES, d), F32)],
            compiler_params=_cparams("parallel", "arbitrary"),
            name="rwkv_proj_long",
        )(*ins)
    row = pl.BlockSpec((tm, d), lambda i: (i, 0))

    def full(a):
        return pl.BlockSpec(a.shape, lambda i: (0,) * a.ndim)

    ins = [x, gn, w["mu8"], w["proj_vec"]] + ws + [w["ones"]]
    return pl.pallas_call(
        functools.partial(_rwkv_proj_kernel, seq_len=seq_len, tm=tm),
        grid=(n // tm,),
        in_specs=[row] + [full(a) for a in ins[1:]] + [row],
        out_specs=[row] * 9,
        out_shape=nd * 9,
        compiler_params=_cparams("parallel"),
        name="rwkv_proj_short",
    )(*ins, shift_rows)


def _scan_kernel(kk_ref, d_ref, k_ref, ka_ref, r_ref, vt_ref, s0_ref, selv_ref, place_ref,
                 ones_ref, y_ref, sout_ref, s_scr, lv_scr, p2_scr, *, steps, nb):
    c = pl.program_id(1)

    @pl.when(c == 0)
    def _():
        s_scr[...] = s0_ref[...]

    ones = ones_ref[...]
    low = lax.broadcasted_iota(jnp.int32, (SCAN_ROWS, LANES), 1) < RW_HEAD
    n_half = -(-steps // SCAN_HALF)
    for b in range(nb):
        vt4 = vt_ref[b].reshape(RW_HEADS // 2, 2, RW_HEAD, LANES)
        ev = vt4[:, 0].reshape(SCAN_ROWS, LANES)
        od = vt4[:, 1].reshape(SCAN_ROWS, LANES)
        lv_scr[b, 0] = jnp.where(low, ev, pltpu.roll(od, RW_HEAD, axis=1)).astype(BF16)
        if n_half > 1:
            lv_scr[b, 1] = jnp.where(low, pltpu.roll(ev, RW_HEAD, axis=1), od).astype(BF16)

    def rows(blk, j):
        return jnp.concatenate(
            [jnp.broadcast_to(blk[j:j + 1, hp * LANES:(hp + 1) * LANES], (RW_HEAD, LANES))
             for hp in range(RW_HEADS // 2)], axis=0)

    for half in range(n_half):
        nst = min(SCAN_HALF, steps - half * SCAN_HALF)

        def body(grp, carry, half=half):
            base = pl.multiple_of(half * SCAN_HALF + grp * SUBLANES, SUBLANES)
            blks = [[ref[b, pl.ds(base, SUBLANES), :]
                     for ref in (kk_ref, d_ref, k_ref, ka_ref, r_ref)] for b in range(nb)]
            for j in range(SUBLANES):
                tt = grp * SUBLANES + j
                for b in range(nb):
                    kk8, d8, k8, ka8, r8 = blks[b]
                    s = s_scr[b]
                    sk = _dot((s * rows(kk8, j)).astype(BF16), ones)
                    vcol = _dot(lv_scr[b, half], selv_ref[tt])
                    s = s * rows(d8, j) - sk * rows(ka8, j) + vcol * rows(k8, j)
                    s_scr[b] = s
                    p2_scr[b, tt] = (s * rows(r8, j)).astype(BF16)
            return carry

        lax.fori_loop(0, nst // SUBLANES, body, 0)

        t0 = half * SCAN_HALF
        for b in range(nb):
            lhs = jnp.concatenate([p2_scr[b, tt] for tt in range(nst)], axis=1)
            yt = _dot(lhs, place_ref[0:nst * LANES, :])
            for q in range(RW_HEADS // 4):
                tr = yt[q * 2 * RW_HEAD:(q + 1) * 2 * RW_HEAD, :].T
                c0 = q * 2 * LANES
                y_ref[b, t0:t0 + nst, c0:c0 + LANES] = tr[0:nst]
                y_ref[b, t0:t0 + nst, c0 + LANES:c0 + 2 * LANES] = tr[SCAN_HALF:SCAN_HALF + nst]

    @pl.when(c == pl.num_programs(1) - 1)
    def _():
        sout_ref[...] = s_scr[...]


def _scan_tables():
    col = jnp.arange(LANES)
    row = jnp.arange(LANES)
    t = jnp.arange(SCAN_HALF)
    selv = ((row[None, :, None] // RW_HEAD == col[None, None, :] // RW_HEAD)
            & (row[None, :, None] % RW_HEAD == t[:, None, None])).astype(BF16)
    place = ((row[None, :, None] // RW_HEAD == col[None, None, :] // RW_HEAD)
             & (col[None, None, :] % RW_HEAD == t[:, None, None])).astype(BF16)
    return selv, place.reshape(SCAN_HALF * LANES, LANES)


def _seg_ones():
    i = jnp.arange(LANES)
    return (i[:, None] // RW_HEAD == i[None, :] // RW_HEAD).astype(BF16)


def _scan_call(kk, dec, k, ka, r, vt, s0, steps):
    b, t, d = kk.shape
    assert steps % SUBLANES == 0 and t % steps == 0
    nb = SCAN_NB if b % SCAN_NB == 0 else 1
    nc = t // steps
    n_half = -(-steps // SCAN_HALF)
    selv, place = _scan_tables()
    ones = _seg_ones()
    rowspec = pl.BlockSpec((nb, steps, d), lambda i, c: (i, c, 0))

    def full(a):
        return pl.BlockSpec(a.shape, lambda i, c: (0,) * a.ndim)

    sspec = pl.BlockSpec((nb, SCAN_ROWS, LANES), lambda i, c: (i, 0, 0))
    return pl.pallas_call(
        functools.partial(_scan_kernel, steps=steps, nb=nb),
        grid=(b // nb, nc),
        in_specs=[rowspec] * 5 + [pl.BlockSpec((nb, d, LANES), lambda i, c: (i, 0, c)), sspec,
                                  full(selv), full(place), full(ones)],
        out_specs=[rowspec, sspec],
        out_shape=[jax.ShapeDtypeStruct((b, t, d), F32),
                   jax.ShapeDtypeStruct((b, SCAN_ROWS, LANES), F32)],
        scratch_shapes=[pltpu.VMEM((nb, SCAN_ROWS, LANES), F32),
                        pltpu.VMEM((nb, n_half, SCAN_ROWS, LANES), BF16),
                        pltpu.VMEM((nb, min(steps, SCAN_HALF), SCAN_ROWS, LANES), BF16)],
        compiler_params=_cparams("parallel", "arbitrary"),
        name="rwkv_scan",
    )(kk, dec, k, ka, r, vt, s0, selv, place, ones)


def _state_to_scan(s):
    b = s.shape[0]
    s = s[:, jnp.array(SCAN_HEAD_ORDER)]
    return s.reshape(b, RW_HEADS // 2, 2, RW_HEAD, RW_HEAD).transpose(0, 1, 3, 2, 4).reshape(
        b, SCAN_ROWS, LANES)


def _state_from_scan(s):
    b = s.shape[0]
    s = s.reshape(b, RW_HEADS // 2, RW_HEAD, 2, RW_HEAD).transpose(0, 1, 3, 2, 4).reshape(
        b, RW_HEADS, RW_HEAD, RW_HEAD)
    return s[:, jnp.array(SCAN_HEAD_ORDER)]


def _rwkv_out_kernel(y_ref, bonus_ref, g_ref, x_ref, vec_ref, wo_ref, ones_ref, o_ref):
    ones = ones_ref[...]
    y = y_ref[...]
    mean = _segsum(y, ones) * (1.0 / RW_HEAD)
    yc = y - mean
    var = _segsum(yc * yc, ones) * (1.0 / RW_HEAD)
    yn = yc * lax.rsqrt(var + RW_GN_EPS) * vec_ref[0:1, :] + vec_ref[1:2, :]
    z = ((yn + bonus_ref[...]) * g_ref[...]).astype(BF16)
    mo = _dot(z, wo_ref[...])
    o_ref[...] = x_ref[...] + _rms(mo, vec_ref[2:3, :])


def _rwkv_out_call(y, bonus, g, x, vec8, wo, ones, tm):
    n, d = y.shape
    row = pl.BlockSpec((tm, d), lambda i: (i, 0))

    def full(a):
        return pl.BlockSpec(a.shape, lambda i: (0,) * a.ndim)

    return pl.pallas_call(
        _rwkv_out_kernel,
        grid=(n // tm,),
        in_specs=[row] * 4 + [full(vec8), full(wo), full(ones)],
        out_specs=row,
        out_shape=jax.ShapeDtypeStruct((n, d), F32),
        compiler_params=_cparams("parallel"),
        name="rwkv_out",
    )(y, bonus, g, x, vec8, wo, ones)


def _gelu_tanh(x):
    return 0.5 * x * (1.0 + jnp.tanh(math.sqrt(2.0 / math.pi) * (x + 0.044715 * (x * x * x))))


def _ffn_kernel(*refs, seq_len, tm):
    if seq_len is None:
        x_ref, g_ref, wup_ref, cw_ref, cb_ref, wdown_ref, xo_ref, hl_ref, carry = refs
    else:
        x_ref, g_ref, wup_ref, cw_ref, cb_ref, wdown_ref, p1_ref, p2_ref, xo_ref, hl_ref = refs
    x = x_ref[...]
    hn = _rms(x, g_ref[0:1, :]).astype(BF16)
    rows = lax.broadcasted_iota(jnp.int32, (tm, FF_CHUNK), 0)
    if seq_len is None:
        @pl.when(pl.program_id(1) == 0)
        def _():
            carry[...] = jnp.zeros(carry.shape, F32)
    else:
        tpos = rows % seq_len

    acc = jnp.zeros((tm, D_MODEL), F32)
    for j in range(D_FF // FF_CHUNK):
        conv = []
        for part in range(2):
            c0 = part * D_FF + j * FF_CHUNK
            cs = slice(c0, c0 + FF_CHUNK)
            h = _dot(hn, wup_ref[:, cs])
            h1 = pltpu.roll(h, 1, axis=0)
            h2 = pltpu.roll(h, 2, axis=0)
            if seq_len is None:
                pm2 = carry[0:1, cs]
                pm1 = carry[1:2, cs]
                h1 = jnp.where(rows == 0, pm1, h1)
                h2 = jnp.where(rows == 0, pm2, jnp.where(rows == 1, pm1, h2))
                carry[0:2, cs] = h[tm - 2:tm, :]
                hl_ref[:, cs] = h[tm - SUBLANES:tm, :]
            else:
                h1 = jnp.where(tpos == 0, p1_ref[:, cs], h1)
                h2 = jnp.where(tpos < 2, p2_ref[:, cs], h2)
                hl_ref[:, cs] = h
            conv.append(cb_ref[0:1, cs] + h2 * cw_ref[0:1, cs] + h1 * cw_ref[1:2, cs]
                        + h * cw_ref[2:3, cs])
        act = (_gelu_tanh(conv[0]) * conv[1]).astype(BF16)
        acc = acc + _dot(act, wdown_ref[j * FF_CHUNK:(j + 1) * FF_CHUNK, :])
    xo_ref[...] = x + _rms(acc, g_ref[1:2, :])


def _ffn_call(x, g2, wup, cw8, cb8, wdown, tm, seq_len=None, prev=None, batch=None):
    n, d = x.shape
    f2 = 2 * D_FF
    kern = functools.partial(_ffn_kernel, seq_len=seq_len, tm=tm)
    if seq_len is None:
        tpb = n // batch // tm
        row = pl.BlockSpec((tm, d), lambda b, j: (b * tpb + j, 0))

        def full(a):
            return pl.BlockSpec(a.shape, lambda b, j: (0,) * a.ndim)

        return pl.pallas_call(
            kern,
            grid=(batch, tpb),
            in_specs=[row, full(g2), full(wup), full(cw8), full(cb8), full(wdown)],
            out_specs=[row, pl.BlockSpec((SUBLANES, f2), lambda b, j: (b * tpb + j, 0))],
            out_shape=[jax.ShapeDtypeStruct((n, d), F32),
                       jax.ShapeDtypeStruct((n // tm * SUBLANES, f2), F32)],
            scratch_shapes=[pltpu.VMEM((SUBLANES, f2), F32)],
            compiler_params=_cparams("parallel", "arbitrary"),
            name="conv_ffn_long",
        )(x, g2, wup, cw8, cb8, wdown)
    p1, p2 = prev
    row = pl.BlockSpec((tm, d), lambda i: (i, 0))
    hrow = pl.BlockSpec((tm, f2), lambda i: (i, 0))

    def full(a):
        return pl.BlockSpec(a.shape, lambda i: (0,) * a.ndim)

    return pl.pallas_call(
        kern,
        grid=(n // tm,),
        in_specs=[row, full(g2), full(wup), full(cw8), full(cb8), full(wdown), hrow, hrow],
        out_specs=[row, hrow],
        out_shape=[jax.ShapeDtypeStruct((n, d), F32), jax.ShapeDtypeStruct((n, f2), F32)],
        compiler_params=_cparams("parallel"),
        name="conv_ffn_short",
    )(x, g2, wup, cw8, cb8, wdown, p1, p2)


def _qkv_kernel(*refs, long_mode, tm, seq):
    if long_mode:
        x_ref, g_ref, w_ref, ct_ref, s1_ref, s2_ref = refs[:6]
        a_refs = refs[6:6 + N_GROUPS]
        kv_refs = refs[6 + N_GROUPS:6 + 2 * N_GROUPS]
        scr = refs[6 + 2 * N_GROUPS:]
    else:
        x_ref, g_ref, w_ref, ct_ref, s1_ref, s2_ref, o_ref = refs
    hn = _rms(x_ref[...], g_ref[...]).astype(BF16)
    ct = ct_ref[...]
    s1 = s1_ref[...]
    s2 = s2_ref[...]
    per_group = ATT_WIDTH // LANES
    per_part = N_GROUPS * per_group
    for c in range(3 * per_part):
        cs = slice(c * LANES, (c + 1) * LANES)
        xb = _dot(hn, w_ref[:, cs])
        part, rem = divmod(c, per_part)
        if part < 2:
            xb = (xb * ct + pltpu.roll(xb, LANES - ROT_DIM // 2, axis=1) * s1
                  + pltpu.roll(xb, ROT_DIM // 2, axis=1) * s2)
        if long_mode:
            gi, cb = divmod(rem, per_group)
            scr[gi][part * per_group + cb] = xb
        else:
            o_ref[:, cs] = xb
    if long_mode:
        for gi in range(N_GROUPS):
            window, dil = ATT_GROUPS[gi]
            kb = min(min(window, seq), tm)
            for cb in range(3 * per_group):
                cs = slice(cb * LANES, (cb + 1) * LANES)
                for res in range(dil):
                    a_refs[gi][0, res, :, cs] = scr[gi][cb, pl.ds(res, tm // dil, stride=dil), :]
                if cb >= per_group:
                    kv_refs[gi][0, :, (cb - per_group) * LANES:(cb - per_group + 1) * LANES] = (
                        scr[gi][cb, tm - kb:tm, :])


def _rope_tables(pos):
    half = ROT_DIM // 2
    inv = jnp.power(ROPE_THETA, -jnp.arange(half, dtype=F32) / half)
    ang = pos.astype(F32)[:, None] * inv[None, :]
    cos, sin = jnp.cos(ang), jnp.sin(ang)
    n = pos.shape[0]
    one = jnp.ones((n, ATT_HEAD_DIM - ROT_DIM), F32)
    zero = jnp.zeros((n, ATT_HEAD_DIM - ROT_DIM), F32)
    zh = jnp.zeros((n, half), F32)
    ct = jnp.concatenate([cos, cos, one], axis=1)
    s1 = jnp.concatenate([-sin, zh, zero], axis=1)
    s2 = jnp.concatenate([zh, sin, zero], axis=1)
    rep = LANES // ATT_HEAD_DIM
    return tuple(jnp.tile(a, (1, rep)) for a in (ct, s1, s2))


def _qkv_call(x, g, wqkv, tables, tm, batch=None):
    n, d = x.shape
    nq = wqkv.shape[1]
    gspec_shape = (1, d)
    if batch is None:
        row = pl.BlockSpec((tm, d), lambda i: (i, 0))
        tab = pl.BlockSpec((tm, LANES), lambda i: (0, 0))
        return pl.pallas_call(
            functools.partial(_qkv_kernel, long_mode=False, tm=tm, seq=None),
            grid=(n // tm,),
            in_specs=[row, pl.BlockSpec(gspec_shape, lambda i: (0, 0)),
                      pl.BlockSpec(wqkv.shape, lambda i: (0, 0)), tab, tab, tab],
            out_specs=pl.BlockSpec((tm, nq), lambda i: (i, 0)),
            out_shape=jax.ShapeDtypeStruct((n, nq), F32),
            compiler_params=_cparams("parallel"),
            name="attn_qkv_rope_short",
        )(x, g.reshape(1, d), wqkv, *tables)
    s = n // batch
    tpb = s // tm
    row = pl.BlockSpec((tm, d), lambda b, j: (b * tpb + j, 0))
    tab = pl.BlockSpec((tm, LANES), lambda b, j: (j, 0))
    a_specs, a_shapes, kv_specs, kv_shapes = [], [], [], []
    for window, dil in ATT_GROUPS:
        a_specs.append(pl.BlockSpec((1, dil, tm // dil, 3 * ATT_WIDTH), lambda b, j: (b, 0, j, 0)))
        a_shapes.append(jax.ShapeDtypeStruct((batch, dil, s // dil, 3 * ATT_WIDTH), F32))
        keep = min(window, s)
        kb = min(keep, tm)
        kv_specs.append(pl.BlockSpec(
            (1, kb, 2 * ATT_WIDTH),
            lambda b, j, kb=kb, keep=keep: (b, jnp.maximum((j + 1) * tm - kb - (s - keep), 0) // kb, 0)))
        kv_shapes.append(jax.ShapeDtypeStruct((batch, keep, 2 * ATT_WIDTH), F32))
    outs = pl.pallas_call(
        functools.partial(_qkv_kernel, long_mode=True, tm=tm, seq=s),
        grid=(batch, tpb),
        in_specs=[row, pl.BlockSpec(gspec_shape, lambda b, j: (0, 0)),
                  pl.BlockSpec(wqkv.shape, lambda b, j: (0, 0)), tab, tab, tab],
        out_specs=a_specs + kv_specs,
        out_shape=a_shapes + kv_shapes,
        scratch_shapes=[pltpu.VMEM((3 * ATT_WIDTH // LANES, tm, LANES), F32)] * N_GROUPS,
        compiler_params=_cparams("parallel", "arbitrary"),
        name="attn_qkv_rope_long",
    )(x, g.reshape(1, d), wqkv, *tables)
    return outs[:N_GROUPS], outs[N_GROUPS:]


def _attn_prompt_kernel(q_ref, kp_ref, kc_ref, vp_ref, vc_ref, o_ref, l_ref):
    n = pl.program_id(2)
    blk = ATT_BLK
    qi = lax.broadcasted_iota(jnp.int32, (blk, 2 * blk), 0)
    kj = lax.broadcasted_iota(jnp.int32, (blk, 2 * blk), 1)
    dist = qi + blk - kj
    valid = (dist >= 0) & (dist <= blk) & (n * blk + kj - blk >= 0)
    q = q_ref[0, 0] * ATT_SCALE
    for h in range(ATT_HEADS):
        hs = slice(h * ATT_HEAD_DIM, (h + 1) * ATT_HEAD_DIM)
        qh = q[:, hs].astype(BF16)
        kh = jnp.concatenate([kp_ref[0, 0, :, hs], kc_ref[0, 0, :, hs]], axis=0).astype(BF16)
        vh = jnp.concatenate([vp_ref[0, 0, :, hs], vc_ref[0, 0, :, hs]], axis=0).astype(BF16)
        s = lax.dot_general(qh, kh, (((1,), (1,)), ((), ())), preferred_element_type=F32)
        s = jnp.where(valid, s, -jnp.inf)
        m = jnp.max(s, axis=-1, keepdims=True)
        p = jnp.exp(s - m)
        l = jnp.sum(p, axis=-1, keepdims=True)
        o = _dot(p.astype(BF16), vh) / l
        o_ref[0, 0, :, hs] = o
        l_ref[0, 0, :, hs] = jnp.broadcast_to(m + jnp.log(l), (blk, ATT_HEAD_DIM))


def _attn_prompt_call(a, gi):
    b, dil, n_sub, _ = a.shape
    nb = n_sub // ATT_BLK
    blkshape = (1, 1, ATT_BLK, ATT_WIDTH)

    def cur(part):
        return pl.BlockSpec(blkshape, lambda i, r, n: (i, r, n, part))

    def prv(part):
        return pl.BlockSpec(blkshape, lambda i, r, n: (i, r, jnp.maximum(n - 1, 0), part))

    ospec = pl.BlockSpec(blkshape, lambda i, r, n: (i, r, n, 0))
    return pl.pallas_call(
        _attn_prompt_kernel,
        grid=(b, dil, nb),
        in_specs=[cur(0), prv(1), cur(1), prv(2), cur(2)],
        out_specs=[ospec, ospec],
        out_shape=[jax.ShapeDtypeStruct((b, dil, n_sub, ATT_WIDTH), F32)] * 2,
        compiler_params=_cparams("parallel", "parallel", "parallel"),
        name=f"attn_prompt_g{gi}",
    )(a, a, a, a, a)


def _head_rows(x, t):
    return jnp.concatenate(
        [x[:, h * ATT_HEAD_DIM:(h + 1) * ATT_HEAD_DIM] for h in range(ATT_HEADS)], axis=0)


def _attn_sample_kernel(qkv_ref, c0_ref, c1_ref, c2_ref, o_ref, *, t, lens, sub):
    caches = (c0_ref, c1_ref, c2_ref)
    nrow = ATT_HEADS * t
    nt = (((1,), (1,)), ((), ()))
    stats = []
    for gi in range(N_GROUPS):
        window, dil = ATT_GROUPS[gi]
        cref = caches[gi]
        ln = lens[gi]
        q = qkv_ref[0, :, gi * ATT_WIDTH:(gi + 1) * ATT_WIDTH] * ATT_SCALE
        knew = qkv_ref[0, :, (N_GROUPS + gi) * ATT_WIDTH:(N_GROUPS + gi + 1) * ATT_WIDTH]
        vnew = qkv_ref[0, :, (2 * N_GROUPS + gi) * ATT_WIDTH:(2 * N_GROUPS + gi + 1) * ATT_WIDTH]
        qx = _head_rows(q, t).astype(BF16)
        kn = _head_rows(knew, t).astype(BF16)
        vn = _head_rows(vnew, t).astype(BF16)
        if sub[gi]:
            kc = cref[0, 0, :, :, 0]
            vc = cref[0, 0, :, :, 1]
            npos = kc.shape[0] * kc.shape[1]
        else:
            kc = cref[0, 0, :, 0]
            vc = cref[0, 0, :, 1]
            npos = kc.shape[0]
        ncol = npos * ATT_HEADS
        kc = kc.reshape(ncol, ATT_HEAD_DIM).astype(BF16)
        vc = vc.reshape(ncol, ATT_HEAD_DIM).astype(BF16)
        sc = lax.dot_general(qx, kc, nt, preferred_element_type=F32)
        sn = lax.dot_general(qx, kn, nt, preferred_element_type=F32)
        rowc = lax.broadcasted_iota(jnp.int32, (nrow, ncol), 0)
        colc = lax.broadcasted_iota(jnp.int32, (nrow, ncol), 1)
        pidx = colc // ATT_HEADS
        pos = (pidx // t) * dil + pidx % t if sub[gi] else pidx
        delta_c = ln + rowc % t - pos
        ok_c = ((colc % ATT_HEADS == rowc // t) & (delta_c <= window) & (delta_c % dil == 0))
        rown = lax.broadcasted_iota(jnp.int32, (nrow, nrow), 0)
        coln = lax.broadcasted_iota(jnp.int32, (nrow, nrow), 1)
        delta_n = rown % t - coln % t
        ok_n = ((coln // t == rown // t) & (delta_n >= 0) & (delta_n <= window)
                & (delta_n % dil == 0))
        sc = jnp.where(ok_c, sc, -jnp.inf)
        sn = jnp.where(ok_n, sn, -jnp.inf)
        m = jnp.maximum(jnp.max(sc, axis=-1, keepdims=True), jnp.max(sn, axis=-1, keepdims=True))
        pc = jnp.exp(sc - m)
        pn = jnp.exp(sn - m)
        l = jnp.sum(pc, axis=-1, keepdims=True) + jnp.sum(pn, axis=-1, keepdims=True)
        og = _dot(pc.astype(BF16), vc) + _dot(pn.astype(BF16), vn)
        stats.append((m + jnp.log(l), l, og))

    mx = functools.reduce(jnp.maximum, [s[0] for s in stats])
    es = [jnp.exp(s[0] - mx) for s in stats]
    den = es[0] + es[1] + es[2]
    full = sum(s[2] * (e / (den * s[1])) for s, e in zip(stats, es))
    for h in range(ATT_HEADS):
        o_ref[0, :, h * ATT_HEAD_DIM:(h + 1) * ATT_HEAD_DIM] = full[h * t:(h + 1) * t, :]


def _attn_sample_call(qkv, caches, b, t):
    nq = qkv.shape[-1]
    views, specs, lens, sub = [], [], [], []
    for (window, dil), c in zip(ATT_GROUPS, caches):
        ln = c.shape[1]
        lens.append(ln)
        if dil >= 2 * t and ln % dil == 0:
            v = c.reshape(1, b, ln // dil, dil, 2, ATT_HEADS, ATT_HEAD_DIM)
            specs.append(pl.BlockSpec((1, 1, ln // dil, t, 2, ATT_HEADS, ATT_HEAD_DIM),
                                      lambda i: (0, i, 0, 0, 0, 0, 0)))
            sub.append(True)
        else:
            v = c.reshape(1, b, ln, 2, ATT_HEADS, ATT_HEAD_DIM)
            specs.append(pl.BlockSpec((1, 1, ln, 2, ATT_HEADS, ATT_HEAD_DIM),
                                      lambda i: (0, i, 0, 0, 0, 0)))
            sub.append(False)
        views.append(v)
    return pl.pallas_call(
        functools.partial(_attn_sample_kernel, t=t, lens=tuple(lens), sub=tuple(sub)),
        grid=(b,),
        in_specs=[pl.BlockSpec((1, t, nq), lambda i: (i, 0, 0))] + specs,
        out_specs=pl.BlockSpec((1, t, ATT_WIDTH), lambda i: (i, 0, 0)),
        out_shape=jax.ShapeDtypeStruct((b, t, ATT_WIDTH), F32),
        compiler_params=_cparams("parallel"),
        name="attn_sample",
    )(qkv.reshape(b, t, nq), *views)


def _attn_out_kernel(*refs, merge, tm):
    if merge:
        o_refs = refs[:N_GROUPS]
        l_refs = refs[N_GROUPS:2 * N_GROUPS]
        x_ref, g_ref, wo_ref, out_ref = refs[2 * N_GROUPS:2 * N_GROUPS + 4]
        scr = refs[2 * N_GROUPS + 4:]
        vals = []
        for gi, ref in enumerate(o_refs + l_refs):
            dil = ATT_GROUPS[gi % N_GROUPS][1]
            if dil == 1:
                vals.append(ref[0, 0])
            else:
                for cb in range(ATT_WIDTH // LANES):
                    for res in range(dil):
                        scr[gi][cb, pl.ds(res, tm // dil, stride=dil), :] = (
                            ref[0, res, :, cb * LANES:(cb + 1) * LANES])
                vals.append(jnp.concatenate([scr[gi][cb] for cb in range(ATT_WIDTH // LANES)], axis=1))
        oa, ob, oc, la, lb, lc = vals
        mx = jnp.maximum(jnp.maximum(la, lb), lc)
        ea, eb, ec = jnp.exp(la - mx), jnp.exp(lb - mx), jnp.exp(lc - mx)
        den = ea + eb + ec
        o = (ea / den) * oa + (eb / den) * ob + (ec / den) * oc
    else:
        o0, x_ref, g_ref, wo_ref, out_ref = refs
        o = o0[...]
    y = _dot(o.astype(BF16), wo_ref[...])
    out_ref[...] = x_ref[...] + _rms(y, g_ref[...])


def _attn_out_call(os_, lses, x, g, wo, tm, batch=None):
    n, d = x.shape
    merge = lses is not None
    kern = functools.partial(_attn_out_kernel, merge=merge, tm=tm)
    if not merge:
        row = pl.BlockSpec((tm, d), lambda i: (i, 0))
        return pl.pallas_call(
            kern,
            grid=(n // tm,),
            in_specs=[pl.BlockSpec((tm, ATT_WIDTH), lambda i: (i, 0)), row,
                      pl.BlockSpec((1, d), lambda i: (0, 0)), pl.BlockSpec(wo.shape, lambda i: (0, 0))],
            out_specs=row,
            out_shape=jax.ShapeDtypeStruct((n, d), F32),
            compiler_params=_cparams("parallel"),
            name="attn_out_short",
        )(os_[0], x, g.reshape(1, d), wo)
    tpb = n // batch // tm
    row = pl.BlockSpec((tm, d), lambda b, j: (b * tpb + j, 0))
    aspecs = [pl.BlockSpec((1, dil, tm // dil, ATT_WIDTH), lambda b, j: (b, 0, j, 0))
              for _, dil in ATT_GROUPS]
    return pl.pallas_call(
        kern,
        grid=(batch, tpb),
        in_specs=aspecs * 2 + [row, pl.BlockSpec((1, d), lambda b, j: (0, 0)),
                               pl.BlockSpec(wo.shape, lambda b, j: (0, 0))],
        out_specs=row,
        out_shape=jax.ShapeDtypeStruct((n, d), F32),
        scratch_shapes=[pltpu.VMEM((ATT_WIDTH // LANES, tm, LANES), F32)] * (2 * N_GROUPS),
        compiler_params=_cparams("parallel", "parallel"),
        name="attn_out_long",
    )(*os_, *lses, x, g.reshape(1, d), wo)


def _pad_rows8(a):
    return jnp.concatenate([a, jnp.zeros((SUBLANES - a.shape[0],) + a.shape[1:], a.dtype)], axis=0)


def _pad_to(a, axis, size):
    padw = [(0, 0)] * a.ndim
    padw[axis] = (0, size - a.shape[axis])
    return jnp.pad(a, padw)


def _prep_weights(norm_g, rw, at_wqkv, at_wo, ff):
    (mu, wr, wk, wv, wo, w0, w1, w2, a0, a1, a2, g1, g2, k_k, k_a, r_k, lnx_g, lnx_b) = [p[0] for p in rw]
    perm = (jnp.array(SCAN_HEAD_ORDER)[:, None] * RW_HEAD + jnp.arange(RW_HEAD)[None, :]).reshape(-1)
    glora = 2 * LANES
    wvp = wv[:, perm].astype(BF16)
    proj_ws = [wr[:, perm].astype(BF16), wk[:, perm].astype(BF16), wvp,
               _pad_to(w1, 1, LANES).astype(BF16), _pad_to(w2[:, perm], 0, LANES).astype(BF16),
               _pad_to(a1, 1, LANES).astype(BF16), _pad_to(a2[:, perm], 0, LANES).astype(BF16),
               _pad_to(g1, 1, glora).astype(BF16), _pad_to(g2, 0, glora).astype(BF16)]
    return dict(
        mu8=_pad_rows8(mu),
        proj_vec=_pad_rows8(jnp.stack([w0[perm], a0[perm], k_k[perm], k_a[perm], r_k.reshape(-1)[perm]])),
        proj_ws=proj_ws,
        wvt=wvp.T,
        out_vec=_pad_rows8(jnp.stack([lnx_g, lnx_b, norm_g[0, 1]])),
        rw_wo=wo.astype(BF16),
        wqkv=at_wqkv[0].astype(BF16),
        at_wo=at_wo[0].astype(BF16),
        ff=[(_pad_rows8(jnp.stack([norm_g[l, 2], norm_g[l, 3]])), ff[0][l].astype(BF16),
             _pad_rows8(ff[1][l]), _pad_rows8(ff[2][l][None]), ff[3][l].astype(BF16))
            for l in range(2)],
        ones=_seg_ones(),
    )


def _run_group(x, pos, shift0, wkv0, kv_bufs, conv0, norm_g, w, tm):
    b, t, d = x.shape
    n = b * t
    tm = min(tm, n)
    long_mode = shift0 is None
    xf = x.reshape(n, d)
    gn0 = norm_g[0, 0].reshape(1, d)
    to3 = lambda a: a.reshape(b, t, d)

    if long_mode:
        r, dec, k, kk, ka, g, bonus, vt, hl = _rwkv_proj_call(xf, gn0, w, min(tm, 256), batch=b)
        new_shift = hl[:, SUBLANES - 1][None]
        s0 = jnp.zeros((b, SCAN_ROWS, LANES), F32)
    else:
        sh = jnp.concatenate([shift0[0][:, None, :], jnp.zeros((b, t - 1, d), F32)], axis=1)
        r, dec, k, kk, ka, g, bonus, v, hn = _rwkv_proj_call(
            xf, gn0, w, tm, seq_len=t, shift_rows=sh.reshape(n, d))
        new_shift = to3(hn)[:, -1][None]
        vt = _pad_to(to3(v).transpose(0, 2, 1), 2, LANES)
        s0 = _state_to_scan(wkv0[0])
    y, s_fin = _scan_call(to3(kk), to3(dec), to3(k), to3(ka), to3(r), vt, s0, min(t, SCAN_CHUNK))
    x1 = _rwkv_out_call(y.reshape(n, d), bonus, g, xf, w["out_vec"], w["rw_wo"], w["ones"], tm)
    new_wkv = _state_from_scan(s_fin)[None]

    def ffn(xin, layer):
        g2, wup, cw8, cb8, wdown = w["ff"][layer]
        if long_mode:
            xo, hl = _ffn_call(xin, g2, wup, cw8, cb8, wdown, tm, batch=b)
            cs = hl.reshape(b, t // tm, SUBLANES, 2 * D_FF)[:, -1, SUBLANES - (CONV_W - 1):]
        else:
            st = conv0[layer]
            z = jnp.zeros((b, t - 1, 2 * D_FF), F32)
            p1 = jnp.concatenate([st[:, 1:2], z], axis=1).reshape(n, 2 * D_FF)
            p2 = jnp.concatenate([st, z[:, 1:]], axis=1).reshape(n, 2 * D_FF)
            xo, hl = _ffn_call(xin, g2, wup, cw8, cb8, wdown, tm, seq_len=t, prev=(p1, p2))
            cs = hl.reshape(b, t, 2 * D_FF)[:, t - (CONV_W - 1):]
        return xo, cs

    x2, conv_a = ffn(x1, 0)

    new_kv = []
    if long_mode:
        a_g, kv_g = _qkv_call(x2, norm_g[1, 0], w["wqkv"], _rope_tables(pos), tm, batch=b)
        for kv in kv_g:
            new_kv.append(kv.reshape(1, b, kv.shape[1], 2, ATT_HEADS, ATT_HEAD_DIM))
        res = [_attn_prompt_call(a_g[gi], gi) for gi in range(N_GROUPS)]
        x3 = _attn_out_call([o for o, _ in res], [l for _, l in res], x2, norm_g[1, 1],
                            w["at_wo"], tm, batch=b)
    else:
        tables = tuple(jnp.tile(a, (tm // t, 1)) for a in _rope_tables(pos))
        qkv = _qkv_call(x2, norm_g[1, 0], w["wqkv"], tables, tm)
        qkv3 = qkv.reshape(b, t, 3 * N_GROUPS, ATT_HEADS, ATT_HEAD_DIM)
        for gi in range(N_GROUPS):
            new_kv.append(jnp.stack([qkv3[:, :, N_GROUPS + gi], qkv3[:, :, 2 * N_GROUPS + gi]],
                                    axis=2)[None])
        o = _attn_sample_call(qkv, [buf[0] for buf in kv_bufs], b, t)
        x3 = _attn_out_call([o.reshape(n, ATT_WIDTH)], None, x2, norm_g[1, 1], w["at_wo"], tm)
    x4, conv_b = ffn(x3, 1)
    return (x4.reshape(b, t, d), new_shift, new_wkv, new_kv, jnp.stack([conv_a, conv_b]))


def kernel(x_prompt, x_sample, state_shift, state_wkv, cache_kv_w128, cache_kv_w512, cache_kv_w2048, state_conv, norm_g, rw_mu, rw_wr, rw_wk, rw_wv, rw_wo, rw_w0, rw_w1, rw_w2, rw_a0, rw_a1, rw_a2, rw_g1, rw_g2, rw_kk, rw_ka, rw_rk, rw_lnx_g, rw_lnx_b, at_wqkv, at_wo, ff_wup, ff_conv_w, ff_conv_b, ff_wdown):
    rw = (rw_mu, rw_wr, rw_wk, rw_wv, rw_wo, rw_w0, rw_w1, rw_w2, rw_a0, rw_a1, rw_a2,
          rw_g1, rw_g2, rw_kk, rw_ka, rw_rk, rw_lnx_g, rw_lnx_b)
    ff = (ff_wup, ff_conv_w, ff_conv_b, ff_wdown)
    w = _prep_weights(norm_g, rw, at_wqkv, at_wo, ff)
    tp = x_prompt.shape[1]
    ts = x_sample.shape[1]
    pos_p = jnp.arange(tp, dtype=jnp.int32)
    y_p, p_shift, p_wkv, p_kv, p_conv = _run_group(
        x_prompt, pos_p, None, None, None, None, norm_g, w, 512)
    pos_s = PAST_LEN + jnp.arange(ts, dtype=jnp.int32)
    y_s, s_shift, s_wkv, s_kv, s_conv = _run_group(
        x_sample, pos_s, state_shift, state_wkv,
        (cache_kv_w128, cache_kv_w512, cache_kv_w2048), state_conv, norm_g, w, 256)
    return (y_p, y_s, p_shift, p_wkv, p_kv[0], p_kv[1], p_kv[2], p_conv,
            s_shift, s_wkv, s_kv[0], s_kv[1], s_kv[2], s_conv)
```

```python
import functools
import math

import jax
import jax.numpy as jnp
from jax import lax
from jax.experimental import pallas as pl
from jax.experimental.pallas import tpu as pltpu

F32 = jnp.float32
BF16 = jnp.bfloat16

D_MODEL = 1024
RW_HEAD = 64
RW_HEADS = D_MODEL // RW_HEAD
RW_GN_EPS = 64e-5
ATT_GROUPS = ((128, 1), (512, 4), (2048, 16))
N_GROUPS = 3
ATT_HEADS = 8
ATT_HEAD_DIM = 64
ATT_WIDTH = ATT_HEADS * ATT_HEAD_DIM
ATT_SCALE = ATT_HEAD_DIM ** -0.5
ATT_BLK = 128
ROT_DIM = ATT_HEAD_DIM // 4
ROPE_THETA = 500000.0
D_FF = 2816
CONV_W = 3
NORM_EPS = 1e-6
PAST_LEN = 2048

LANES = 128
SUBLANES = 8
FF_CHUNK = 256
FF_DOWN_CHUNKS = 4
SCAN_CHUNK = 128
SCAN_HALF = 64
SCAN_ROWS = RW_HEADS // 2 * RW_HEAD
SCAN_NB = 2
VMEM_LIMIT = 56 * 1024 * 1024


def _cparams(*sem):
    return pltpu.CompilerParams(dimension_semantics=sem, vmem_limit_bytes=VMEM_LIMIT)


def _dot(a, b):
    return jnp.dot(a, b, preferred_element_type=F32)


def _rms(x, g):
    ms = jnp.mean(x * x, axis=-1, keepdims=True)
    return x * lax.rsqrt(ms + NORM_EPS) * g


def _segsum(x, ones):
    outs = []
    for c in range(x.shape[1] // LANES):
        xc = x[:, c * LANES:(c + 1) * LANES]
        hi = xc.astype(BF16)
        lo = (xc - hi.astype(F32)).astype(BF16)
        outs.append(_dot(hi, ones) + _dot(lo, ones))
    return jnp.concatenate(outs, axis=1)


def _scan_to_natural_heads(x):
    low = lax.broadcasted_iota(jnp.int32, (x.shape[0], LANES), 1) < RW_HEAD
    outs = []
    for c in range(0, x.shape[1] // LANES, 2):
        a = x[:, c * LANES:(c + 1) * LANES]
        b = x[:, (c + 1) * LANES:(c + 2) * LANES]
        outs.append(jnp.where(low, a, pltpu.roll(b, RW_HEAD, axis=1)))
        outs.append(jnp.where(low, pltpu.roll(a, RW_HEAD, axis=1), b))
    return jnp.concatenate(outs, axis=1)


def _rwkv_proj_kernel(*refs, seq_len, tm):
    if seq_len is None:
        (x_ref, gn_ref, mu_ref, vec_ref, wr_ref, wk_ref, wv_ref, wvt_ref, w1_ref, w2_ref, a1_ref,
         a2_ref, g1_ref, g2_ref, ones_ref,
         r_ref, d_ref, k_ref, kk_ref, ka_ref, g_ref, bonus_ref, vt_ref, hl_ref, carry) = refs
    else:
        (x_ref, gn_ref, mu_ref, vec_ref, wr_ref, wk_ref, wv_ref, w1_ref, w2_ref, a1_ref,
         a2_ref, g1_ref, g2_ref, ones_ref, sh_ref,
         r_ref, d_ref, k_ref, kk_ref, ka_ref, g_ref, bonus_ref, v_ref, hn_ref) = refs
    hn = _rms(x_ref[...], gn_ref[...])
    rolled = pltpu.roll(hn, 1, axis=0)
    rows = lax.broadcasted_iota(jnp.int32, (tm, D_MODEL), 0)
    if seq_len is None:
        @pl.when(pl.program_id(1) == 0)
        def _():
            carry[...] = jnp.zeros(carry.shape, F32)

        prev = jnp.where(rows == 0, carry[0:1, :], rolled)
        carry[0:1, :] = hn[tm - 1:tm, :]
        hl_ref[0] = hn[tm - SUBLANES:tm, :]
    else:
        prev = jnp.where(rows % seq_len == 0, sh_ref[...], rolled)
        hn_ref[...] = hn
    xx = prev - hn

    def mix(i):
        return (hn + xx * mu_ref[i:i + 1, :]).astype(BF16)

    w0 = vec_ref[0:1, :]
    a0 = vec_ref[1:2, :]
    k_k = vec_ref[2:3, :]
    k_a = vec_ref[3:4, :]
    r_k = vec_ref[4:5, :]

    r = _dot(mix(0), wr_ref[...])
    k = _dot(mix(2), wk_ref[...])
    xv = mix(3)
    v = _dot(xv, wv_ref[...])
    wl = _dot(jnp.tanh(_dot(mix(1), w1_ref[...])).astype(BF16), w2_ref[...])
    z = -(w0 + wl)
    softplus = jnp.maximum(z, 0.0) + jnp.log1p(jnp.exp(-jnp.abs(z)))
    w = -softplus - 0.5
    decay = jnp.exp(-jnp.exp(w))
    al = _dot(_dot(mix(4), a1_ref[...]).astype(BF16), a2_ref[...])
    a = jax.nn.sigmoid(a0 + al)
    g = _dot(jax.nn.sigmoid(_dot(mix(5), g1_ref[...])).astype(BF16), g2_ref[...])

    ones = ones_ref[...]
    kk = k * k_k
    nrm = jnp.sqrt(_segsum(kk * kk, ones))
    kk = kk / jnp.maximum(nrm, 1e-12)
    k2 = k * (1.0 + (a - 1.0) * k_a)

    r_ref[...] = r
    d_ref[...] = decay
    k_ref[...] = k2
    kk_ref[...] = kk
    ka_ref[...] = kk * a
    g_ref[...] = g
    bonus_ref[...] = _scan_to_natural_heads(_segsum(r * k2 * r_k, ones) * v)
    if seq_len is None:
        vt_ref[0] = lax.dot_general(wvt_ref[...], xv, (((1,), (1,)), ((), ())),
                                    preferred_element_type=F32)
    else:
        v_ref[...] = v


def _rwkv_proj_call(x, gn, w, tm, batch=None, seq_len=None, shift_rows=None):
    n, d = x.shape
    nd = [jax.ShapeDtypeStruct((n, d), F32)]
    ws = w["proj_ws"]
    if seq_len is None:
        t = n // batch
        tpb = t // tm
        row = pl.BlockSpec((tm, d), lambda b, j: (b * tpb + j, 0))

        def full(a):
            return pl.BlockSpec(a.shape, lambda b, j: (0,) * a.ndim)

        ins = [x, gn, w["mu8"], w["proj_vec"], ws[0], ws[1], ws[2], w["wvt"]] + ws[3:] + [w["ones"]]
        return pl.pallas_call(
            functools.partial(_rwkv_proj_kernel, seq_len=None, tm=tm),
            grid=(batch, tpb),
            in_specs=[row] + [full(a) for a in ins[1:]],
            out_specs=[row] * 7 + [pl.BlockSpec((1, d, tm), lambda b, j: (b, 0, j)),
                                   pl.BlockSpec((1, SUBLANES, d), lambda b, j: (b, 0, 0))],
            out_shape=nd * 7 + [jax.ShapeDtypeStruct((batch, d, t), F32),
                                jax.ShapeDtypeStruct((batch, SUBLANES, d), F32)],
            scratch_shapes=[pltpu.VMEM((SUBLANES, d), F32)],
            compiler_params=_cparams("parallel", "arbitrary"),
            name="rwkv_proj_long",
        )(*ins)
    row = pl.BlockSpec((tm, d), lambda i: (i, 0))

    def full(a):
        return pl.BlockSpec(a.shape, lambda i: (0,) * a.ndim)

    ins = [x, gn, w["mu8"], w["proj_vec"]] + ws + [w["ones"]]
    return pl.pallas_call(
        functools.partial(_rwkv_proj_kernel, seq_len=seq_len, tm=tm),
        grid=(n // tm,),
        in_specs=[row] + [full(a) for a in ins[1:]] + [row],
        out_specs=[row] * 9,
        out_shape=nd * 9,
        compiler_params=_cparams("parallel"),
        name="rwkv_proj_short",
    )(*ins, shift_rows)


def _scan_kernel(kk_ref, d_ref, k_ref, ka_ref, r_ref, vt_ref, s0_ref, selv_ref, place_ref,
                 ones_ref, y_ref, sout_ref, s_scr, lv_scr, yt_scr, *, steps, nb):
    c = pl.program_id(1)

    @pl.when(c == 0)
    def _():
        s_scr[...] = s0_ref[...]

    ones = ones_ref[...]
    low = lax.broadcasted_iota(jnp.int32, (SCAN_ROWS, LANES), 1) < RW_HEAD
    n_half = -(-steps // SCAN_HALF)
    for b in range(nb):
        vt4 = vt_ref[b].reshape(RW_HEADS // 2, 2, RW_HEAD, LANES)
        ev = vt4[:, 0].reshape(SCAN_ROWS, LANES)
        od = vt4[:, 1].reshape(SCAN_ROWS, LANES)
        lv_scr[b, 0] = jnp.where(low, ev, pltpu.roll(od, RW_HEAD, axis=1)).astype(BF16)
        if n_half > 1:
            lv_scr[b, 1] = jnp.where(low, pltpu.roll(ev, RW_HEAD, axis=1), od).astype(BF16)

    def rows(blk, j):
        return jnp.concatenate(
            [jnp.broadcast_to(blk[j:j + 1, hp * LANES:(hp + 1) * LANES], (RW_HEAD, LANES))
             for hp in range(RW_HEADS // 2)], axis=0)

    for half in range(n_half):
        nst = min(SCAN_HALF, steps - half * SCAN_HALF)
        yt_scr[...] = jnp.zeros(yt_scr.shape, F32)

        def body(grp, carry, half=half):
            base = pl.multiple_of(half * SCAN_HALF + grp * SUBLANES, SUBLANES)
            blks = [[ref[b, pl.ds(base, SUBLANES), :]
                     for ref in (kk_ref, d_ref, k_ref, ka_ref, r_ref)] for b in range(nb)]
            p2 = [[] for _ in range(nb)]
            for j in range(SUBLANES):
                for b in range(nb):
                    kk8, d8, k8, ka8, r8 = blks[b]
                    s = s_scr[b]
                    sk = _dot((s * rows(kk8, j)).astype(BF16), ones)
                    vcol = _dot(lv_scr[b, half], selv_ref[grp * SUBLANES + j])
                    s = s * rows(d8, j) - sk * rows(ka8, j) + vcol * rows(k8, j)
                    s_scr[b] = s
                    p2[b].append((s * rows(r8, j)).astype(BF16))
            prow = pl.multiple_of(grp * SUBLANES * LANES, SUBLANES * LANES)
            pl8 = place_ref[pl.ds(prow, SUBLANES * LANES), :]
            for b in range(nb):
                yt_scr[b] += _dot(jnp.concatenate(p2[b], axis=1), pl8)
            return carry

        lax.fori_loop(0, nst // SUBLANES, body, 0)

        t0 = half * SCAN_HALF
        for b in range(nb):
            yt = yt_scr[b]
            for q in range(RW_HEADS // 4):
                tr = yt[q * 2 * RW_HEAD:(q + 1) * 2 * RW_HEAD, :].T
                c0 = q * 2 * LANES
                y_ref[b, t0:t0 + nst, c0:c0 + LANES] = tr[0:nst]
                y_ref[b, t0:t0 + nst, c0 + LANES:c0 + 2 * LANES] = tr[SCAN_HALF:SCAN_HALF + nst]

    @pl.when(c == pl.num_programs(1) - 1)
    def _():
        sout_ref[...] = s_scr[...]


def _scan_tables():
    col = jnp.arange(LANES)
    row = jnp.arange(LANES)
    t = jnp.arange(SCAN_HALF)
    selv = ((row[None, :, None] // RW_HEAD == col[None, None, :] // RW_HEAD)
            & (row[None, :, None] % RW_HEAD == t[:, None, None])).astype(BF16)
    place = ((row[None, :, None] // RW_HEAD == col[None, None, :] // RW_HEAD)
             & (col[None, None, :] % RW_HEAD == t[:, None, None])).astype(BF16)
    return selv, place.reshape(SCAN_HALF * LANES, LANES)


def _seg_ones():
    i = jnp.arange(LANES)
    return (i[:, None] // RW_HEAD == i[None, :] // RW_HEAD).astype(BF16)


def _scan_call(kk, dec, k, ka, r, vt, s0, steps):
    b, t, d = kk.shape
    assert steps % SUBLANES == 0 and t % steps == 0
    nb = SCAN_NB if b % SCAN_NB == 0 else 1
    nc = t // steps
    n_half = -(-steps // SCAN_HALF)
    selv, place = _scan_tables()
    ones = _seg_ones()
    rowspec = pl.BlockSpec((nb, steps, d), lambda i, c: (i, c, 0))

    def full(a):
        return pl.BlockSpec(a.shape, lambda i, c: (0,) * a.ndim)

    sspec = pl.BlockSpec((nb, SCAN_ROWS, LANES), lambda i, c: (i, 0, 0))
    return pl.pallas_call(
        functools.partial(_scan_kernel, steps=steps, nb=nb),
        grid=(b // nb, nc),
        in_specs=[rowspec] * 5 + [pl.BlockSpec((nb, d, LANES), lambda i, c: (i, 0, c)), sspec,
                                  full(selv), full(place), full(ones)],
        out_specs=[rowspec, sspec],
        out_shape=[jax.ShapeDtypeStruct((b, t, d), F32),
                   jax.ShapeDtypeStruct((b, SCAN_ROWS, LANES), F32)],
        scratch_shapes=[pltpu.VMEM((nb, SCAN_ROWS, LANES), F32),
                        pltpu.VMEM((nb, n_half, SCAN_ROWS, LANES), BF16),
                        pltpu.VMEM((nb, SCAN_ROWS, LANES), F32)],
        compiler_params=_cparams("parallel", "arbitrary"),
        name="rwkv_scan",
    )(kk, dec, k, ka, r, vt, s0, selv, place, ones)


def _to_scan_heads(a):
    lead = a.shape[:-1]
    return a.reshape(lead + (RW_HEADS // 4, 2, 2, RW_HEAD)).swapaxes(-3, -2).reshape(
        lead + (D_MODEL,))


def _state_to_scan(s):
    b = s.shape[0]
    s = s.reshape(b, RW_HEADS // 4, 2, 2, RW_HEAD, RW_HEAD)
    return s.transpose(0, 1, 3, 4, 2, 5).reshape(b, SCAN_ROWS, LANES)


def _state_from_scan(s):
    b = s.shape[0]
    s = s.reshape(b, RW_HEADS // 4, 2, RW_HEAD, 2, RW_HEAD)
    return s.transpose(0, 1, 4, 2, 3, 5).reshape(b, RW_HEADS, RW_HEAD, RW_HEAD)


def _rwkv_out_kernel(y_ref, bonus_ref, g_ref, x_ref, vec_ref, wo_ref, ones_ref, o_ref):
    ones = ones_ref[...]
    y = y_ref[...]
    mean = _segsum(y, ones) * (1.0 / RW_HEAD)
    yc = y - mean
    var = _segsum(yc * yc, ones) * (1.0 / RW_HEAD)
    yn = yc * lax.rsqrt(var + RW_GN_EPS) * vec_ref[0:1, :] + vec_ref[1:2, :]
    z = ((yn + bonus_ref[...]) * g_ref[...]).astype(BF16)
    mo = _dot(z, wo_ref[...])
    o_ref[...] = x_ref[...] + _rms(mo, vec_ref[2:3, :])


def _rwkv_out_call(y, bonus, g, x, vec8, wo, ones, tm):
    n, d = y.shape
    row = pl.BlockSpec((tm, d), lambda i: (i, 0))

    def full(a):
        return pl.BlockSpec(a.shape, lambda i: (0,) * a.ndim)

    return pl.pallas_call(
        _rwkv_out_kernel,
        grid=(n // tm,),
        in_specs=[row] * 4 + [full(vec8), full(wo), full(ones)],
        out_specs=row,
        out_shape=jax.ShapeDtypeStruct((n, d), F32),
        compiler_params=_cparams("parallel"),
        name="rwkv_out",
    )(y, bonus, g, x, vec8, wo, ones)


def _gelu_tanh(x):
    return 0.5 * x * (1.0 + jnp.tanh(math.sqrt(2.0 / math.pi) * (x + 0.044715 * (x * x * x))))


def _ffn_kernel(*refs, seq_len, tm):
    if seq_len is None:
        (x_ref, g_ref, wup_ref, cw_ref, cb_ref, wdown_ref, xo_ref, hl_ref,
         carry, hs_scr, act_scr) = refs
    else:
        x_ref, g_ref, wup_ref, cw_ref, cb_ref, wdown_ref, p1_ref, p2_ref, xo_ref, hl_ref = refs
    x = x_ref[...]
    hn = _rms(x, g_ref[0:1, :]).astype(BF16)
    n_chunks = D_FF // FF_CHUNK
    if seq_len is None:
        @pl.when(pl.program_id(1) == 0)
        def _():
            carry[...] = jnp.zeros(carry.shape, F32)
    else:
        rows = lax.broadcasted_iota(jnp.int32, (tm, FF_CHUNK), 0)
        tpos = rows % seq_len

    acc = jnp.zeros((tm, D_MODEL), F32)
    seg0 = 0
    for j in range(n_chunks):
        conv = []
        for part in range(2):
            c0 = part * D_FF + j * FF_CHUNK
            cs = slice(c0, c0 + FF_CHUNK)
            h = _dot(hn, wup_ref[:, cs])
            if seq_len is None:
                buf = (j % 2) * 2 + part
                hs_scr[buf, 0:SUBLANES, :] = carry[:, cs]
                hs_scr[buf, SUBLANES:SUBLANES + tm, :] = h
                carry[:, cs] = h[tm - SUBLANES:tm, :]
                hl_ref[:, cs] = h[tm - SUBLANES:tm, :]
                h1 = hs_scr[buf, SUBLANES - 1:SUBLANES - 1 + tm, :]
                h2 = hs_scr[buf, SUBLANES - 2:SUBLANES - 2 + tm, :]
            else:
                h1 = jnp.where(tpos == 0, p1_ref[:, cs], pltpu.roll(h, 1, axis=0))
                h2 = jnp.where(tpos < 2, p2_ref[:, cs], pltpu.roll(h, 2, axis=0))
                hl_ref[:, cs] = h
            conv.append(cb_ref[0:1, cs] + h2 * cw_ref[0:1, cs] + h1 * cw_ref[1:2, cs]
                        + h * cw_ref[2:3, cs])
        act = (_gelu_tanh(conv[0]) * conv[1]).astype(BF16)
        if seq_len is None:
            act_scr[:, j * FF_CHUNK:(j + 1) * FF_CHUNK] = act
            if (j + 1) % FF_DOWN_CHUNKS == 0 or j == n_chunks - 1:
                seg = slice(seg0 * FF_CHUNK, (j + 1) * FF_CHUNK)
                acc = acc + _dot(act_scr[:, seg], wdown_ref[seg, :])
                seg0 = j + 1
        else:
            acc = acc + _dot(act, wdown_ref[j * FF_CHUNK:(j + 1) * FF_CHUNK, :])
    xo_ref[...] = x + _rms(acc, g_ref[1:2, :])


def _ffn_call(x, g2, wup, cw8, cb8, wdown, tm, seq_len=None, prev=None, batch=None):
    n, d = x.shape
    f2 = 2 * D_FF
    kern = functools.partial(_ffn_kernel, seq_len=seq_len, tm=tm)
    if seq_len is None:
        tpb = n // batch // tm
        row = pl.BlockSpec((tm, d), lambda b, j: (b * tpb + j, 0))

        def full(a):
            return pl.BlockSpec(a.shape, lambda b, j: (0,) * a.ndim)

        return pl.pallas_call(
            kern,
            grid=(batch, tpb),
            in_specs=[row, full(g2), full(wup), full(cw8), full(cb8), full(wdown)],
            out_specs=[row, pl.BlockSpec((SUBLANES, f2), lambda b, j: (b * tpb + j, 0))],
            out_shape=[jax.ShapeDtypeStruct((n, d), F32),
                       jax.ShapeDtypeStruct((n // tm * SUBLANES, f2), F32)],
            scratch_shapes=[pltpu.VMEM((SUBLANES, f2), F32),
                            pltpu.VMEM((4, tm + SUBLANES, FF_CHUNK), F32),
                            pltpu.VMEM((tm, D_FF), BF16)],
            compiler_params=_cparams("parallel", "arbitrary"),
            name="conv_ffn_long",
        )(x, g2, wup, cw8, cb8, wdown)
    p1, p2 = prev
    row = pl.BlockSpec((tm, d), lambda i: (i, 0))
    hrow = pl.BlockSpec((tm, f2), lambda i: (i, 0))

    def full(a):
        return pl.BlockSpec(a.shape, lambda i: (0,) * a.ndim)

    return pl.pallas_call(
        kern,
        grid=(n // tm,),
        in_specs=[row, full(g2), full(wup), full(cw8), full(cb8), full(wdown), hrow, hrow],
        out_specs=[row, hrow],
        out_shape=[jax.ShapeDtypeStruct((n, d), F32), jax.ShapeDtypeStruct((n, f2), F32)],
        compiler_params=_cparams("parallel"),
        name="conv_ffn_short",
    )(x, g2, wup, cw8, cb8, wdown, p1, p2)


def _qkv_kernel(*refs, long_mode, tm, seq):
    if long_mode:
        x_ref, g_ref, w_ref, ct_ref, s1_ref, s2_ref = refs[:6]
        a_refs = refs[6:6 + N_GROUPS]
        kv_refs = refs[6 + N_GROUPS:6 + 2 * N_GROUPS]
        scr = refs[6 + 2 * N_GROUPS:]
    else:
        x_ref, g_ref, w_ref, ct_ref, s1_ref, s2_ref, o_ref = refs
    hn = _rms(x_ref[...], g_ref[...]).astype(BF16)
    ct = ct_ref[...]
    s1 = s1_ref[...]
    s2 = s2_ref[...]
    per_group = ATT_WIDTH // LANES
    per_part = N_GROUPS * per_group
    for c in range(3 * per_part):
        cs = slice(c * LANES, (c + 1) * LANES)
        if c % 2 == 0:
            xb2 = _dot(hn, w_ref[:, c * LANES:(c + 2) * LANES])
        xb = xb2[:, (c % 2) * LANES:(c % 2 + 1) * LANES]
        part, rem = divmod(c, per_part)
        if part < 2:
            xb = (xb * ct + pltpu.roll(xb, LANES - ROT_DIM // 2, axis=1) * s1
                  + pltpu.roll(xb, ROT_DIM // 2, axis=1) * s2)
        if long_mode:
            gi, cb = divmod(rem, per_group)
            scr[gi][part * per_group + cb] = xb
        else:
            o_ref[:, cs] = xb
    if long_mode:
        for gi in range(N_GROUPS):
            window, dil = ATT_GROUPS[gi]
            kb = min(min(window, seq), tm)
            for cb in range(3 * per_group):
                cs = slice(cb * LANES, (cb + 1) * LANES)
                for res in range(dil):
                    a_refs[gi][0, res, :, cs] = scr[gi][cb, pl.ds(res, tm // dil, stride=dil), :]
                if cb >= per_group:
                    kv_refs[gi][0, :, (cb - per_group) * LANES:(cb - per_group + 1) * LANES] = (
                        scr[gi][cb, tm - kb:tm, :])


def _rope_tables(pos):
    half = ROT_DIM // 2
    inv = jnp.power(ROPE_THETA, -jnp.arange(half, dtype=F32) / half)
    ang = pos.astype(F32)[:, None] * inv[None, :]
    cos, sin = jnp.cos(ang), jnp.sin(ang)
    n = pos.shape[0]
    one = jnp.ones((n, ATT_HEAD_DIM - ROT_DIM), F32)
    zero = jnp.zeros((n, ATT_HEAD_DIM - ROT_DIM), F32)
    zh = jnp.zeros((n, half), F32)
    ct = jnp.concatenate([cos, cos, one], axis=1)
    s1 = jnp.concatenate([-sin, zh, zero], axis=1)
    s2 = jnp.concatenate([zh, sin, zero], axis=1)
    rep = LANES // ATT_HEAD_DIM
    return tuple(jnp.tile(a, (1, rep)) for a in (ct, s1, s2))


def _qkv_call(x, g, wqkv, tables, tm, batch=None):
    n, d = x.shape
    nq = wqkv.shape[1]
    gspec_shape = (1, d)
    if batch is None:
        row = pl.BlockSpec((tm, d), lambda i: (i, 0))
        tab = pl.BlockSpec((tm, LANES), lambda i: (0, 0))
        return pl.pallas_call(
            functools.partial(_qkv_kernel, long_mode=False, tm=tm, seq=None),
            grid=(n // tm,),
            in_specs=[row, pl.BlockSpec(gspec_shape, lambda i: (0, 0)),
                      pl.BlockSpec(wqkv.shape, lambda i: (0, 0)), tab, tab, tab],
            out_specs=pl.BlockSpec((tm, nq), lambda i: (i, 0)),
            out_shape=jax.ShapeDtypeStruct((n, nq), F32),
            compiler_params=_cparams("parallel"),
            name="attn_qkv_rope_short",
        )(x, g.reshape(1, d), wqkv, *tables)
    s = n // batch
    tpb = s // tm
    row = pl.BlockSpec((tm, d), lambda b, j: (b * tpb + j, 0))
    tab = pl.BlockSpec((tm, LANES), lambda b, j: (j, 0))
    a_specs, a_shapes, kv_specs, kv_shapes = [], [], [], []
    for window, dil in ATT_GROUPS:
        a_specs.append(pl.BlockSpec((1, dil, tm // dil, 3 * ATT_WIDTH), lambda b, j: (b, 0, j, 0)))
        a_shapes.append(jax.ShapeDtypeStruct((batch, dil, s // dil, 3 * ATT_WIDTH), F32))
        keep = min(window, s)
        kb = min(keep, tm)
        kv_specs.append(pl.BlockSpec(
            (1, kb, 2 * ATT_WIDTH),
            lambda b, j, kb=kb, keep=keep: (b, jnp.maximum((j + 1) * tm - kb - (s - keep), 0) // kb, 0)))
        kv_shapes.append(jax.ShapeDtypeStruct((batch, keep, 2 * ATT_WIDTH), F32))
    outs = pl.pallas_call(
        functools.partial(_qkv_kernel, long_mode=True, tm=tm, seq=s),
        grid=(batch, tpb),
        in_specs=[row, pl.BlockSpec(gspec_shape, lambda b, j: (0, 0)),
                  pl.BlockSpec(wqkv.shape, lambda b, j: (0, 0)), tab, tab, tab],
        out_specs=a_specs + kv_specs,
        out_shape=a_shapes + kv_shapes,
        scratch_shapes=[pltpu.VMEM((3 * ATT_WIDTH // LANES, tm, LANES), F32)] * N_GROUPS,
        compiler_params=_cparams("parallel", "arbitrary"),
        name="attn_qkv_rope_long",
    )(x, g.reshape(1, d), wqkv, *tables)
    return outs[:N_GROUPS], outs[N_GROUPS:]


def _attn_prompt_kernel(q_ref, kp_ref, kc_ref, vp_ref, vc_ref, o_ref, l_ref):
    n = pl.program_id(2)
    blk = ATT_BLK
    qi = lax.broadcasted_iota(jnp.int32, (blk, 2 * blk), 0)
    kj = lax.broadcasted_iota(jnp.int32, (blk, 2 * blk), 1)
    dist = qi + blk - kj
    valid = (dist >= 0) & (dist <= blk) & (n * blk + kj - blk >= 0)
    q = q_ref[0, 0] * ATT_SCALE
    low = lax.broadcasted_iota(jnp.int32, (blk, LANES), 1) < ATT_HEAD_DIM
    for hp in range(ATT_WIDTH // LANES):
        cs = slice(hp * LANES, (hp + 1) * LANES)
        qp = q[:, cs]
        kp = jnp.concatenate([kp_ref[0, 0, :, cs], kc_ref[0, 0, :, cs]], axis=0).astype(BF16)
        vp = jnp.concatenate([vp_ref[0, 0, :, cs], vc_ref[0, 0, :, cs]], axis=0).astype(BF16)
        outs = []
        for par in range(2):
            qm = jnp.where(low if par == 0 else jnp.logical_not(low), qp, 0.0).astype(BF16)
            s = lax.dot_general(qm, kp, (((1,), (1,)), ((), ())), preferred_element_type=F32)
            s = jnp.where(valid, s, -jnp.inf)
            m = jnp.max(s, axis=-1, keepdims=True)
            p = jnp.exp(s - m)
            l = jnp.sum(p, axis=-1, keepdims=True)
            outs.append((_dot(p.astype(BF16), vp) / l,
                         jnp.broadcast_to(m + jnp.log(l), (blk, LANES))))
        o_ref[0, 0, :, cs] = jnp.where(low, outs[0][0], outs[1][0])
        l_ref[0, 0, :, cs] = jnp.where(low, outs[0][1], outs[1][1])


def _attn_prompt_call(a, gi):
    b, dil, n_sub, _ = a.shape
    nb = n_sub // ATT_BLK
    blkshape = (1, 1, ATT_BLK, ATT_WIDTH)

    def cur(part):
        return pl.BlockSpec(blkshape, lambda i, r, n: (i, r, n, part))

    def prv(part):
        return pl.BlockSpec(blkshape, lambda i, r, n: (i, r, jnp.maximum(n - 1, 0), part))

    ospec = pl.BlockSpec(blkshape, lambda i, r, n: (i, r, n, 0))
    return pl.pallas_call(
        _attn_prompt_kernel,
        grid=(b, dil, nb),
        in_specs=[cur(0), prv(1), cur(1), prv(2), cur(2)],
        out_specs=[ospec, ospec],
        out_shape=[jax.ShapeDtypeStruct((b, dil, n_sub, ATT_WIDTH), F32)] * 2,
        compiler_params=_cparams("parallel", "parallel", "parallel"),
        name=f"attn_prompt_g{gi}",
    )(a, a, a, a, a)


def _attn_sample_kernel(qkv_ref, c0_ref, c1_ref, c2_ref, o_ref, *, t):
    caches = (c0_ref, c1_ref, c2_ref)
    nrow = ATT_HEADS * t
    rowi = lax.broadcasted_iota(jnp.int32, (nrow, ATT_WIDTH), 0)
    coli = lax.broadcasted_iota(jnp.int32, (nrow, ATT_WIDTH), 1)
    headmask = (rowi // t) == (coli // ATT_HEAD_DIM)
    pad = jnp.zeros((LANES - t, ATT_WIDTH), F32)
    nt = (((1,), (1,)), ((), ()))
    stats = []
    for gi in range(N_GROUPS):
        window, dil = ATT_GROUPS[gi]
        cref = caches[gi]
        ln = cref.shape[-1]
        q = qkv_ref[0, :, gi * ATT_WIDTH:(gi + 1) * ATT_WIDTH] * ATT_SCALE
        knew = qkv_ref[0, :, (N_GROUPS + gi) * ATT_WIDTH:(N_GROUPS + gi + 1) * ATT_WIDTH]
        vnew = qkv_ref[0, :, (2 * N_GROUPS + gi) * ATT_WIDTH:(2 * N_GROUPS + gi + 1) * ATT_WIDTH]
        qexp = jnp.where(headmask, jnp.concatenate([q] * ATT_HEADS, axis=0), 0.0).astype(BF16)
        kn = jnp.concatenate([knew, pad], axis=0).astype(BF16)
        vn = jnp.concatenate([vnew, pad], axis=0).astype(BF16)
        sc = _dot(qexp, cref[0, 0].astype(BF16))
        sn = lax.dot_general(qexp, kn, nt, preferred_element_type=F32)
        qi_c = lax.broadcasted_iota(jnp.int32, (nrow, ln), 0) % t
        pos_c = lax.broadcasted_iota(jnp.int32, (nrow, ln), 1)
        delta_c = ln + qi_c - pos_c
        ok_c = (delta_c <= window) & (delta_c % dil == 0)
        qi_n = lax.broadcasted_iota(jnp.int32, (nrow, LANES), 0) % t
        j_n = lax.broadcasted_iota(jnp.int32, (nrow, LANES), 1)
        delta_n = qi_n - j_n
        ok_n = (j_n < t) & (delta_n >= 0) & (delta_n <= window) & (delta_n % dil == 0)
        sc = jnp.where(ok_c, sc, -jnp.inf)
        sn = jnp.where(ok_n, sn, -jnp.inf)
        m = jnp.maximum(jnp.max(sc, axis=-1, keepdims=True), jnp.max(sn, axis=-1, keepdims=True))
        pc = jnp.exp(sc - m)
        pn = jnp.exp(sn - m)
        l = jnp.sum(pc, axis=-1, keepdims=True) + jnp.sum(pn, axis=-1, keepdims=True)
        og = (lax.dot_general(pc.astype(BF16), cref[0, 1].astype(BF16), nt,
                              preferred_element_type=F32)
              + _dot(pn.astype(BF16), vn))
        stats.append((m + jnp.log(l), l, og))

    mx = functools.reduce(jnp.maximum, [s[0] for s in stats])
    es = [jnp.exp(s[0] - mx) for s in stats]
    den = es[0] + es[1] + es[2]
    full = sum(s[2] * (e / (den * s[1])) for s, e in zip(stats, es))
    for h in range(ATT_HEADS):
        hs = slice(h * ATT_HEAD_DIM, (h + 1) * ATT_HEAD_DIM)
        o_ref[0, :, hs] = full[h * t:(h + 1) * t, hs]


def _attn_sample_call(qkv, caches, b, t):
    nq = qkv.shape[-1]
    views = [c.transpose(0, 2, 3, 4, 1).reshape(b, 2, ATT_WIDTH, c.shape[1]) for c in caches]
    return pl.pallas_call(
        functools.partial(_attn_sample_kernel, t=t),
        grid=(b,),
        in_specs=[pl.BlockSpec((1, t, nq), lambda i: (i, 0, 0))]
                 + [pl.BlockSpec((1,) + v.shape[1:], lambda i: (i, 0, 0, 0)) for v in views],
        out_specs=pl.BlockSpec((1, t, ATT_WIDTH), lambda i: (i, 0, 0)),
        out_shape=jax.ShapeDtypeStruct((b, t, ATT_WIDTH), F32),
        compiler_params=_cparams("parallel"),
        name="attn_sample",
    )(qkv.reshape(b, t, nq), *views)


def _attn_out_kernel(*refs, merge, tm):
    if merge:
        o_refs = refs[:N_GROUPS]
        l_refs = refs[N_GROUPS:2 * N_GROUPS]
        x_ref, g_ref, wo_ref, out_ref = refs[2 * N_GROUPS:2 * N_GROUPS + 4]
        scr = refs[2 * N_GROUPS + 4:]
        vals = []
        for gi, ref in enumerate(o_refs + l_refs):
            dil = ATT_GROUPS[gi % N_GROUPS][1]
            if dil == 1:
                vals.append(ref[0, 0])
            else:
                for cb in range(ATT_WIDTH // LANES):
                    for res in range(dil):
                        scr[gi][cb, pl.ds(res, tm // dil, stride=dil), :] = (
                            ref[0, res, :, cb * LANES:(cb + 1) * LANES])
                vals.append(jnp.concatenate([scr[gi][cb] for cb in range(ATT_WIDTH // LANES)], axis=1))
        oa, ob, oc, la, lb, lc = vals
        mx = jnp.maximum(jnp.maximum(la, lb), lc)
        ea, eb, ec = jnp.exp(la - mx), jnp.exp(lb - mx), jnp.exp(lc - mx)
        den = ea + eb + ec
        o = (ea / den) * oa + (eb / den) * ob + (ec / den) * oc
    else:
        o0, x_ref, g_ref, wo_ref, out_ref = refs
        o = o0[...]
    y = _dot(o.astype(BF16), wo_ref[...])
    out_ref[...] = x_ref[...] + _rms(y, g_ref[...])


def _attn_out_call(os_, lses, x, g, wo, tm, batch=None):
    n, d = x.shape
    merge = lses is not None
    kern = functools.partial(_attn_out_kernel, merge=merge, tm=tm)
    if not merge:
        row = pl.BlockSpec((tm, d), lambda i: (i, 0))
        return pl.pallas_call(
            kern,
            grid=(n // tm,),
            in_specs=[pl.BlockSpec((tm, ATT_WIDTH), lambda i: (i, 0)), row,
                      pl.BlockSpec((1, d), lambda i: (0, 0)), pl.BlockSpec(wo.shape, lambda i: (0, 0))],
            out_specs=row,
            out_shape=jax.ShapeDtypeStruct((n, d), F32),
            compiler_params=_cparams("parallel"),
            name="attn_out_short",
        )(os_[0], x, g.reshape(1, d), wo)
    tpb = n // batch // tm
    row = pl.BlockSpec((tm, d), lambda b, j: (b * tpb + j, 0))
    aspecs = [pl.BlockSpec((1, dil, tm // dil, ATT_WIDTH), lambda b, j: (b, 0, j, 0))
              for _, dil in ATT_GROUPS]
    return pl.pallas_call(
        kern,
        grid=(batch, tpb),
        in_specs=aspecs * 2 + [row, pl.BlockSpec((1, d), lambda b, j: (0, 0)),
                               pl.BlockSpec(wo.shape, lambda b, j: (0, 0))],
        out_specs=row,
        out_shape=jax.ShapeDtypeStruct((n, d), F32),
        scratch_shapes=[pltpu.VMEM((ATT_WIDTH // LANES, tm, LANES), F32)] * (2 * N_GROUPS),
        compiler_params=_cparams("parallel", "parallel"),
        name="attn_out_long",
    )(*os_, *lses, x, g.reshape(1, d), wo)


def _pad_rows8(a):
    return jnp.concatenate([a, jnp.zeros((SUBLANES - a.shape[0],) + a.shape[1:], a.dtype)], axis=0)


def _pad_to(a, axis, size):
    padw = [(0, 0)] * a.ndim
    padw[axis] = (0, size - a.shape[axis])
    return jnp.pad(a, padw)


def _prep_weights(norm_g, rw, at_wqkv, at_wo, ff):
    (mu, wr, wk, wv, wo, w0, w1, w2, a0, a1, a2, g1, g2, k_k, k_a, r_k, lnx_g, lnx_b) = [p[0] for p in rw]
    glora = 2 * LANES
    sc = lambda a: _to_scan_heads(a.astype(BF16))
    wvp = sc(wv)
    proj_ws = [sc(wr), sc(wk), wvp,
               _pad_to(w1, 1, LANES).astype(BF16), _pad_to(sc(w2), 0, LANES),
               _pad_to(a1, 1, LANES).astype(BF16), _pad_to(sc(a2), 0, LANES),
               _pad_to(g1, 1, glora).astype(BF16), _pad_to(g2, 0, glora).astype(BF16)]
    return dict(
        mu8=_pad_rows8(mu),
        proj_vec=_pad_rows8(_to_scan_heads(jnp.stack([w0, a0, k_k, k_a, r_k.reshape(-1)]))),
        proj_ws=proj_ws,
        wvt=wvp.T,
        out_vec=_pad_rows8(jnp.stack([lnx_g, lnx_b, norm_g[0, 1]])),
        rw_wo=wo.astype(BF16),
        wqkv=at_wqkv[0].astype(BF16),
        at_wo=at_wo[0].astype(BF16),
        ff=[(_pad_rows8(jnp.stack([norm_g[l, 2], norm_g[l, 3]])), ff[0][l].astype(BF16),
             _pad_rows8(ff[1][l]), _pad_rows8(ff[2][l][None]), ff[3][l].astype(BF16))
            for l in range(2)],
        ones=_seg_ones(),
    )


def _run_group(x, pos, shift0, wkv0, kv_bufs, conv0, norm_g, w, tm):
    b, t, d = x.shape
    n = b * t
    tm = min(tm, n)
    long_mode = shift0 is None
    xf = x.reshape(n, d)
    gn0 = norm_g[0, 0].reshape(1, d)
    to3 = lambda a: a.reshape(b, t, d)

    if long_mode:
        r, dec, k, kk, ka, g, bonus, vt, hl = _rwkv_proj_call(xf, gn0, w, min(tm, 256), batch=b)
        new_shift = hl[:, SUBLANES - 1][None]
        s0 = jnp.zeros((b, SCAN_ROWS, LANES), F32)
    else:
        sh = jnp.concatenate([shift0[0][:, None, :], jnp.zeros((b, t - 1, d), F32)], axis=1)
        r, dec, k, kk, ka, g, bonus, v, hn = _rwkv_proj_call(
            xf, gn0, w, tm, seq_len=t, shift_rows=sh.reshape(n, d))
        new_shift = to3(hn)[:, -1][None]
        vt = _pad_to(to3(v).transpose(0, 2, 1), 2, LANES)
        s0 = _state_to_scan(wkv0[0])
    y, s_fin = _scan_call(to3(kk), to3(dec), to3(k), to3(ka), to3(r), vt, s0, min(t, SCAN_CHUNK))
    x1 = _rwkv_out_call(y.reshape(n, d), bonus, g, xf, w["out_vec"], w["rw_wo"], w["ones"], tm)
    new_wkv = _state_from_scan(s_fin)[None]

    def ffn(xin, layer):
        g2, wup, cw8, cb8, wdown = w["ff"][layer]
        if long_mode:
            xo, hl = _ffn_call(xin, g2, wup, cw8, cb8, wdown, tm, batch=b)
            cs = hl.reshape(b, t // tm, SUBLANES, 2 * D_FF)[:, -1, SUBLANES - (CONV_W - 1):]
        else:
            st = conv0[layer]
            z = jnp.zeros((b, t - 1, 2 * D_FF), F32)
            p1 = jnp.concatenate([st[:, 1:2], z], axis=1).reshape(n, 2 * D_FF)
            p2 = jnp.concatenate([st, z[:, 1:]], axis=1).reshape(n, 2 * D_FF)
            xo, hl = _ffn_call(xin, g2, wup, cw8, cb8, wdown, tm, seq_len=t, prev=(p1, p2))
            cs = hl.reshape(b, t, 2 * D_FF)[:, t - (CONV_W - 1):]
        return xo, cs

    x2, conv_a = ffn(x1, 0)

    new_kv = []
    if long_mode:
        a_g, kv_g = _qkv_call(x2, norm_g[1, 0], w["wqkv"], _rope_tables(pos), tm, batch=b)
        for kv in kv_g:
            new_kv.append(kv.reshape(1, b, kv.shape[1], 2, ATT_HEADS, ATT_HEAD_DIM))
        res = [_attn_prompt_call(a_g[gi], gi) for gi in range(N_GROUPS)]
        x3 = _attn_out_call([o for o, _ in res], [l for _, l in res], x2, norm_g[1, 1],
                            w["at_wo"], tm, batch=b)
    else:
        tables = tuple(jnp.tile(a, (tm // t, 1)) for a in _rope_tables(pos))
        qkv = _qkv_call(x2, norm_g[1, 0], w["wqkv"], tables, tm)
        qkv3 = qkv.reshape(b, t, 3 * N_GROUPS, ATT_HEADS, ATT_HEAD_DIM)
        for gi in range(N_GROUPS):
            new_kv.append(jnp.stack([qkv3[:, :, N_GROUPS + gi], qkv3[:, :, 2 * N_GROUPS + gi]],
                                    axis=2)[None])
        o = _attn_sample_call(qkv, [buf[0] for buf in kv_bufs], b, t)
        x3 = _attn_out_call([o.reshape(n, ATT_WIDTH)], None, x2, norm_g[1, 1], w["at_wo"], tm)
    x4, conv_b = ffn(x3, 1)
    return (x4.reshape(b, t, d), new_shift, new_wkv, new_kv, jnp.stack([conv_a, conv_b]))


def kernel(x_prompt, x_sample, state_shift, state_wkv, cache_kv_w128, cache_kv_w512, cache_kv_w2048, state_conv, norm_g, rw_mu, rw_wr, rw_wk, rw_wv, rw_wo, rw_w0, rw_w1, rw_w2, rw_a0, rw_a1, rw_a2, rw_g1, rw_g2, rw_kk, rw_ka, rw_rk, rw_lnx_g, rw_lnx_b, at_wqkv, at_wo, ff_wup, ff_conv_w, ff_conv_b, ff_wdown):
    rw = (rw_mu, rw_wr, rw_wk, rw_wv, rw_wo, rw_w0, rw_w1, rw_w2, rw_a0, rw_a1, rw_a2,
          rw_g1, rw_g2, rw_kk, rw_ka, rw_rk, rw_lnx_g, rw_lnx_b)
    ff = (ff_wup, ff_conv_w, ff_conv_b, ff_wdown)
    w = _prep_weights(norm_g, rw, at_wqkv, at_wo, ff)
    tp = x_prompt.shape[1]
    ts = x_sample.shape[1]
    pos_p = jnp.arange(tp, dtype=jnp.int32)
    y_p, p_shift, p_wkv, p_kv, p_conv = _run_group(
        x_prompt, pos_p, None, None, None, None, norm_g, w, 512)
    pos_s = PAST_LEN + jnp.arange(ts, dtype=jnp.int32)
    y_s, s_shift, s_wkv, s_kv, s_conv = _run_group(
        x_sample, pos_s, state_shift, state_wkv,
        (cache_kv_w128, cache_kv_w512, cache_kv_w2048), state_conv, norm_g, w, 256)
    return (y_p, y_s, p_shift, p_wkv, p_kv[0], p_kv[1], p_kv[2], p_conv,
            s_shift, s_wkv, s_kv[0], s_kv[1], s_kv[2], s_conv)
```

```python
import functools
import math

import jax
import jax.numpy as jnp
from jax import lax
from jax.experimental import pallas as pl
from jax.experimental.pallas import tpu as pltpu

F32 = jnp.float32
BF16 = jnp.bfloat16

D_MODEL = 1024
RW_HEAD = 64
RW_HEADS = D_MODEL // RW_HEAD
RW_GN_EPS = 64e-5
ATT_GROUPS = ((128, 1), (512, 4), (2048, 16))
N_GROUPS = 3
ATT_HEADS = 8
ATT_HEAD_DIM = 64
ATT_WIDTH = ATT_HEADS * ATT_HEAD_DIM
ATT_SCALE = ATT_HEAD_DIM ** -0.5
ATT_BLK = 128
ATT_STEP_BLOCKS = 4
ROT_DIM = ATT_HEAD_DIM // 4
ROPE_THETA = 500000.0
D_FF = 2816
CONV_W = 3
NORM_EPS = 1e-6
PAST_LEN = 2048

LANES = 128
SUBLANES = 8
FF_CHUNK = 256
FF_DOWN_CHUNKS = 4
FF_LONG_TILE = 512
SCAN_CHUNK = 128
SCAN_HALF = 64
SCAN_ROWS = RW_HEADS // 2 * RW_HEAD
SCAN_NB = 2
VMEM_LIMIT = 56 * 1024 * 1024


def _cparams(*sem):
    return pltpu.CompilerParams(dimension_semantics=sem, vmem_limit_bytes=VMEM_LIMIT)


def _dot(a, b):
    return jnp.dot(a, b, preferred_element_type=F32)


def _rms(x, g):
    ms = jnp.mean(x * x, axis=-1, keepdims=True)
    return x * lax.rsqrt(ms + NORM_EPS) * g


def _segsum(x, ones):
    outs = []
    for c in range(x.shape[1] // LANES):
        xc = x[:, c * LANES:(c + 1) * LANES]
        hi = xc.astype(BF16)
        lo = (xc - hi.astype(F32)).astype(BF16)
        outs.append(_dot(hi, ones) + _dot(lo, ones))
    return jnp.concatenate(outs, axis=1)


def _scan_to_natural_heads(x):
    low = lax.broadcasted_iota(jnp.int32, (x.shape[0], LANES), 1) < RW_HEAD
    outs = []
    for c in range(0, x.shape[1] // LANES, 2):
        a = x[:, c * LANES:(c + 1) * LANES]
        b = x[:, (c + 1) * LANES:(c + 2) * LANES]
        outs.append(jnp.where(low, a, pltpu.roll(b, RW_HEAD, axis=1)))
        outs.append(jnp.where(low, pltpu.roll(a, RW_HEAD, axis=1), b))
    return jnp.concatenate(outs, axis=1)


def _rwkv_proj_kernel(*refs, seq_len, tm):
    if seq_len is None:
        (x_ref, gn_ref, mu_ref, vec_ref, wr_ref, wk_ref, wv_ref, w1_ref, w2_ref, a1_ref,
         a2_ref, g1_ref, g2_ref, ones_ref,
         r_ref, d_ref, k_ref, kk_ref, ka_ref, g_ref, bonus_ref, v_ref, hl_ref, carry) = refs
    else:
        (x_ref, gn_ref, mu_ref, vec_ref, wr_ref, wk_ref, wv_ref, w1_ref, w2_ref, a1_ref,
         a2_ref, g1_ref, g2_ref, ones_ref, sh_ref,
         r_ref, d_ref, k_ref, kk_ref, ka_ref, g_ref, bonus_ref, v_ref, hn_ref) = refs
    hn = _rms(x_ref[...], gn_ref[...])
    rolled = pltpu.roll(hn, 1, axis=0)
    rows = lax.broadcasted_iota(jnp.int32, (tm, D_MODEL), 0)
    if seq_len is None:
        @pl.when(pl.program_id(1) == 0)
        def _():
            carry[...] = jnp.zeros(carry.shape, F32)

        prev = jnp.where(rows == 0, carry[0:1, :], rolled)
        carry[0:1, :] = hn[tm - 1:tm, :]
        hl_ref[0] = hn[tm - SUBLANES:tm, :]
    else:
        prev = jnp.where(rows % seq_len == 0, sh_ref[...], rolled)
        hn_ref[...] = hn
    xx = prev - hn

    def mix(i):
        return (hn + xx * mu_ref[i:i + 1, :]).astype(BF16)

    w0 = vec_ref[0:1, :]
    a0 = vec_ref[1:2, :]
    k_k = vec_ref[2:3, :]
    k_a = vec_ref[3:4, :]
    r_k = vec_ref[4:5, :]

    r = _dot(mix(0), wr_ref[...])
    k = _dot(mix(2), wk_ref[...])
    v = _dot(mix(3), wv_ref[...])
    wl = _dot(jnp.tanh(_dot(mix(1), w1_ref[...])).astype(BF16), w2_ref[...])
    z = -(w0 + wl)
    softplus = jnp.maximum(z, 0.0) + jnp.log1p(jnp.exp(-jnp.abs(z)))
    w = -softplus - 0.5
    decay = jnp.exp(-jnp.exp(w))
    al = _dot(_dot(mix(4), a1_ref[...]).astype(BF16), a2_ref[...])
    a = jax.nn.sigmoid(a0 + al)
    g = _dot(jax.nn.sigmoid(_dot(mix(5), g1_ref[...])).astype(BF16), g2_ref[...])

    ones = ones_ref[...]
    kk = k * k_k
    nrm = jnp.sqrt(_segsum(kk * kk, ones))
    kk = kk / jnp.maximum(nrm, 1e-12)
    k2 = k * (1.0 + (a - 1.0) * k_a)

    r_ref[...] = r
    d_ref[...] = decay
    k_ref[...] = k2
    kk_ref[...] = kk
    ka_ref[...] = kk * a
    g_ref[...] = g
    bonus_ref[...] = _scan_to_natural_heads(_segsum(r * k2 * r_k, ones) * v)
    v_ref[...] = v


def _rwkv_proj_call(x, gn, w, tm, batch=None, seq_len=None, shift_rows=None):
    n, d = x.shape
    nd = [jax.ShapeDtypeStruct((n, d), F32)]
    ws = w["proj_ws"]
    if seq_len is None:
        t = n // batch
        tpb = t // tm
        row = pl.BlockSpec((tm, d), lambda b, j: (b * tpb + j, 0))

        def full(a):
            return pl.BlockSpec(a.shape, lambda b, j: (0,) * a.ndim, pipeline_mode=pl.Buffered(1))

        ins = [x, gn, w["mu8"], w["proj_vec"]] + ws + [w["ones"]]
        return pl.pallas_call(
            functools.partial(_rwkv_proj_kernel, seq_len=None, tm=tm),
            grid=(batch, tpb),
            in_specs=[row] + [full(a) for a in ins[1:]],
            out_specs=[row] * 8 + [pl.BlockSpec((1, SUBLANES, d), lambda b, j: (b, 0, 0))],
            out_shape=nd * 8 + [jax.ShapeDtypeStruct((batch, SUBLANES, d), F32)],
            scratch_shapes=[pltpu.VMEM((SUBLANES, d), F32)],
            compiler_params=_cparams("parallel", "arbitrary"),
            name="rwkv_proj_long",
        )(*ins)
    row = pl.BlockSpec((tm, d), lambda i: (i, 0))

    def full(a):
        return pl.BlockSpec(a.shape, lambda i: (0,) * a.ndim)

    ins = [x, gn, w["mu8"], w["proj_vec"]] + ws + [w["ones"]]
    return pl.pallas_call(
        functools.partial(_rwkv_proj_kernel, seq_len=seq_len, tm=tm),
        grid=(n // tm,),
        in_specs=[row] + [full(a) for a in ins[1:]] + [row],
        out_specs=[row] * 9,
        out_shape=nd * 9,
        compiler_params=_cparams("parallel"),
        name="rwkv_proj_short",
    )(*ins, shift_rows)


def _scan_kernel(kk_ref, d_ref, k_ref, ka_ref, r_ref, v_ref, s0_ref, selv_ref, place_ref,
                 ones_ref, y_ref, sout_ref, s_scr, lv_scr, yt_scr, *, steps, nb):
    c = pl.program_id(1)

    @pl.when(c == 0)
    def _():
        s_scr[...] = s0_ref[...]

    ones = ones_ref[...]
    low = lax.broadcasted_iota(jnp.int32, (SCAN_ROWS, LANES), 1) < RW_HEAD
    n_half = -(-steps // SCAN_HALF)
    for b in range(nb):
        vb = v_ref[b]
        if steps < LANES:
            vb = jnp.concatenate([vb, jnp.zeros((LANES - steps, D_MODEL), F32)], axis=0)
        tr = [vb[:, hp * LANES:(hp + 1) * LANES].T for hp in range(RW_HEADS // 2)]
        ev = jnp.concatenate([x[0:RW_HEAD] for x in tr], axis=0)
        od = jnp.concatenate([x[RW_HEAD:2 * RW_HEAD] for x in tr], axis=0)
        lv_scr[b, 0] = jnp.where(low, ev, pltpu.roll(od, RW_HEAD, axis=1)).astype(BF16)
        if n_half > 1:
            lv_scr[b, 1] = jnp.where(low, pltpu.roll(ev, RW_HEAD, axis=1), od).astype(BF16)

    def rows(blk, j):
        return jnp.concatenate(
            [jnp.broadcast_to(blk[j:j + 1, hp * LANES:(hp + 1) * LANES], (RW_HEAD, LANES))
             for hp in range(RW_HEADS // 2)], axis=0)

    for half in range(n_half):
        nst = min(SCAN_HALF, steps - half * SCAN_HALF)
        yt_scr[...] = jnp.zeros(yt_scr.shape, F32)

        def body(grp, carry, half=half):
            base = pl.multiple_of(half * SCAN_HALF + grp * SUBLANES, SUBLANES)
            blks = [[ref[b, pl.ds(base, SUBLANES), :]
                     for ref in (kk_ref, d_ref, k_ref, ka_ref, r_ref)] for b in range(nb)]
            p2 = [[] for _ in range(nb)]
            for j in range(SUBLANES):
                for b in range(nb):
                    kk8, d8, k8, ka8, r8 = blks[b]
                    s = s_scr[b]
                    sk = _dot((s * rows(kk8, j)).astype(BF16), ones)
                    vcol = _dot(lv_scr[b, half], selv_ref[grp * SUBLANES + j])
                    s = s * rows(d8, j) - sk * rows(ka8, j) + vcol * rows(k8, j)
                    s_scr[b] = s
                    p2[b].append((s * rows(r8, j)).astype(BF16))
            prow = pl.multiple_of(grp * SUBLANES * LANES, SUBLANES * LANES)
            pl8 = place_ref[pl.ds(prow, SUBLANES * LANES), :]
            for b in range(nb):
                yt_scr[b] += _dot(jnp.concatenate(p2[b], axis=1), pl8)
            return carry

        lax.fori_loop(0, nst // SUBLANES, body, 0)

        t0 = half * SCAN_HALF
        for b in range(nb):
            yt = yt_scr[b]
            for q in range(RW_HEADS // 4):
                tr = yt[q * 2 * RW_HEAD:(q + 1) * 2 * RW_HEAD, :].T
                c0 = q * 2 * LANES
                y_ref[b, t0:t0 + nst, c0:c0 + LANES] = tr[0:nst]
                y_ref[b, t0:t0 + nst, c0 + LANES:c0 + 2 * LANES] = tr[SCAN_HALF:SCAN_HALF + nst]

    @pl.when(c == pl.num_programs(1) - 1)
    def _():
        sout_ref[...] = s_scr[...]


def _scan_tables():
    col = jnp.arange(LANES)
    row = jnp.arange(LANES)
    t = jnp.arange(SCAN_HALF)
    selv = ((row[None, :, None] // RW_HEAD == col[None, None, :] // RW_HEAD)
            & (row[None, :, None] % RW_HEAD == t[:, None, None])).astype(BF16)
    place = ((row[None, :, None] // RW_HEAD == col[None, None, :] // RW_HEAD)
             & (col[None, None, :] % RW_HEAD == t[:, None, None])).astype(BF16)
    return selv, place.reshape(SCAN_HALF * LANES, LANES)


def _seg_ones():
    i = jnp.arange(LANES)
    return (i[:, None] // RW_HEAD == i[None, :] // RW_HEAD).astype(BF16)


def _scan_call(kk, dec, k, ka, r, v, s0, steps):
    b, t, d = kk.shape
    assert steps % SUBLANES == 0 and t % steps == 0
    nb = SCAN_NB if b % SCAN_NB == 0 else 1
    nc = t // steps
    n_half = -(-steps // SCAN_HALF)
    selv, place = _scan_tables()
    ones = _seg_ones()
    rowspec = pl.BlockSpec((nb, steps, d), lambda i, c: (i, c, 0))

    def full(a):
        return pl.BlockSpec(a.shape, lambda i, c: (0,) * a.ndim)

    sspec = pl.BlockSpec((nb, SCAN_ROWS, LANES), lambda i, c: (i, 0, 0))
    return pl.pallas_call(
        functools.partial(_scan_kernel, steps=steps, nb=nb),
        grid=(b // nb, nc),
        in_specs=[rowspec] * 6 + [sspec, full(selv), full(place), full(ones)],
        out_specs=[rowspec, sspec],
        out_shape=[jax.ShapeDtypeStruct((b, t, d), F32),
                   jax.ShapeDtypeStruct((b, SCAN_ROWS, LANES), F32)],
        scratch_shapes=[pltpu.VMEM((nb, SCAN_ROWS, LANES), F32),
                        pltpu.VMEM((nb, n_half, SCAN_ROWS, LANES), BF16),
                        pltpu.VMEM((nb, SCAN_ROWS, LANES), F32)],
        compiler_params=_cparams("parallel", "arbitrary"),
        name="rwkv_scan",
    )(kk, dec, k, ka, r, v, s0, selv, place, ones)


def _to_scan_heads(a):
    lead = a.shape[:-1]
    return a.reshape(lead + (RW_HEADS // 4, 2, 2, RW_HEAD)).swapaxes(-3, -2).reshape(
        lead + (D_MODEL,))


def _state_to_scan(s):
    b = s.shape[0]
    s = s.reshape(b, RW_HEADS // 4, 2, 2, RW_HEAD, RW_HEAD)
    return s.transpose(0, 1, 3, 4, 2, 5).reshape(b, SCAN_ROWS, LANES)


def _state_from_scan(s):
    b = s.shape[0]
    s = s.reshape(b, RW_HEADS // 4, 2, RW_HEAD, 2, RW_HEAD)
    return s.transpose(0, 1, 4, 2, 3, 5).reshape(b, RW_HEADS, RW_HEAD, RW_HEAD)


def _rwkv_out_kernel(y_ref, bonus_ref, g_ref, x_ref, vec_ref, wo_ref, ones_ref, o_ref):
    ones = ones_ref[...]
    y = y_ref[...]
    mean = _segsum(y, ones) * (1.0 / RW_HEAD)
    yc = y - mean
    var = _segsum(yc * yc, ones) * (1.0 / RW_HEAD)
    yn = yc * lax.rsqrt(var + RW_GN_EPS) * vec_ref[0:1, :] + vec_ref[1:2, :]
    z = ((yn + bonus_ref[...]) * g_ref[...]).astype(BF16)
    mo = _dot(z, wo_ref[...])
    o_ref[...] = x_ref[...] + _rms(mo, vec_ref[2:3, :])


def _rwkv_out_call(y, bonus, g, x, vec8, wo, ones, tm):
    n, d = y.shape
    row = pl.BlockSpec((tm, d), lambda i: (i, 0))

    def full(a):
        return pl.BlockSpec(a.shape, lambda i: (0,) * a.ndim)

    return pl.pallas_call(
        _rwkv_out_kernel,
        grid=(n // tm,),
        in_specs=[row] * 4 + [full(vec8), full(wo), full(ones)],
        out_specs=row,
        out_shape=jax.ShapeDtypeStruct((n, d), F32),
        compiler_params=_cparams("parallel"),
        name="rwkv_out",
    )(y, bonus, g, x, vec8, wo, ones)


def _gelu_tanh(x):
    return 0.5 * x * (1.0 + jnp.tanh(math.sqrt(2.0 / math.pi) * (x + 0.044715 * (x * x * x))))


def _ffn_kernel(*refs, seq_len, tm):
    if seq_len is None:
        (x_ref, g_ref, wup_ref, cw_ref, cb_ref, wdown_ref, xo_ref, hl_ref,
         carry, hs_scr, act_scr) = refs
    else:
        x_ref, g_ref, wup_ref, cw_ref, cb_ref, wdown_ref, p1_ref, p2_ref, xo_ref, hl_ref = refs
    x = x_ref[...]
    hn = _rms(x, g_ref[0:1, :]).astype(BF16)
    n_chunks = D_FF // FF_CHUNK
    if seq_len is None:
        @pl.when(pl.program_id(1) == 0)
        def _():
            carry[...] = jnp.zeros(carry.shape, F32)
    else:
        rows = lax.broadcasted_iota(jnp.int32, (tm, FF_CHUNK), 0)
        tpos = rows % seq_len

    acc = jnp.zeros((tm, D_MODEL), F32)
    seg0 = 0
    for j in range(n_chunks):
        conv = []
        for part in range(2):
            c0 = part * D_FF + j * FF_CHUNK
            cs = slice(c0, c0 + FF_CHUNK)
            h = _dot(hn, wup_ref[:, cs])
            if seq_len is None:
                buf = (j % 2) * 2 + part
                hs_scr[buf, 0:SUBLANES, :] = carry[:, cs]
                hs_scr[buf, SUBLANES:SUBLANES + tm, :] = h
                carry[:, cs] = h[tm - SUBLANES:tm, :]
                hl_ref[:, cs] = h[tm - SUBLANES:tm, :]
                h1 = hs_scr[buf, SUBLANES - 1:SUBLANES - 1 + tm, :]
                h2 = hs_scr[buf, SUBLANES - 2:SUBLANES - 2 + tm, :]
            else:
                h1 = jnp.where(tpos == 0, p1_ref[:, cs], pltpu.roll(h, 1, axis=0))
                h2 = jnp.where(tpos < 2, p2_ref[:, cs], pltpu.roll(h, 2, axis=0))
                hl_ref[:, cs] = h
            conv.append(cb_ref[0:1, cs] + h2 * cw_ref[0:1, cs] + h1 * cw_ref[1:2, cs]
                        + h * cw_ref[2:3, cs])
        act = (_gelu_tanh(conv[0]) * conv[1]).astype(BF16)
        if seq_len is None:
            act_scr[:, j * FF_CHUNK:(j + 1) * FF_CHUNK] = act
            if (j + 1) % FF_DOWN_CHUNKS == 0 or j == n_chunks - 1:
                seg = slice(seg0 * FF_CHUNK, (j + 1) * FF_CHUNK)
                acc = acc + _dot(act_scr[:, seg], wdown_ref[seg, :])
                seg0 = j + 1
        else:
            acc = acc + _dot(act, wdown_ref[j * FF_CHUNK:(j + 1) * FF_CHUNK, :])
    xo_ref[...] = x + _rms(acc, g_ref[1:2, :])


def _ffn_call(x, g2, wup, cw8, cb8, wdown, tm, seq_len=None, prev=None, batch=None):
    n, d = x.shape
    f2 = 2 * D_FF
    kern = functools.partial(_ffn_kernel, seq_len=seq_len, tm=tm)
    if seq_len is None:
        tpb = n // batch // tm
        row = pl.BlockSpec((tm, d), lambda b, j: (b * tpb + j, 0))

        def full(a):
            return pl.BlockSpec(a.shape, lambda b, j: (0,) * a.ndim, pipeline_mode=pl.Buffered(1))

        return pl.pallas_call(
            kern,
            grid=(batch, tpb),
            in_specs=[row, full(g2), full(wup), full(cw8), full(cb8), full(wdown)],
            out_specs=[row, pl.BlockSpec((SUBLANES, f2), lambda b, j: (b * tpb + j, 0))],
            out_shape=[jax.ShapeDtypeStruct((n, d), F32),
                       jax.ShapeDtypeStruct((n // tm * SUBLANES, f2), F32)],
            scratch_shapes=[pltpu.VMEM((SUBLANES, f2), F32),
                            pltpu.VMEM((4, tm + SUBLANES, FF_CHUNK), F32),
                            pltpu.VMEM((tm, D_FF), BF16)],
            compiler_params=_cparams("parallel", "arbitrary"),
            name="conv_ffn_long",
        )(x, g2, wup, cw8, cb8, wdown)
    p1, p2 = prev
    row = pl.BlockSpec((tm, d), lambda i: (i, 0))
    hrow = pl.BlockSpec((tm, f2), lambda i: (i, 0))

    def full(a):
        return pl.BlockSpec(a.shape, lambda i: (0,) * a.ndim)

    return pl.pallas_call(
        kern,
        grid=(n // tm,),
        in_specs=[row, full(g2), full(wup), full(cw8), full(cb8), full(wdown), hrow, hrow],
        out_specs=[row, hrow],
        out_shape=[jax.ShapeDtypeStruct((n, d), F32), jax.ShapeDtypeStruct((n, f2), F32)],
        compiler_params=_cparams("parallel"),
        name="conv_ffn_short",
    )(x, g2, wup, cw8, cb8, wdown, p1, p2)


def _qkv_kernel(*refs, long_mode, tm, seq):
    if long_mode:
        x_ref, g_ref, w_ref, ct_ref, s1_ref, s2_ref = refs[:6]
        a_refs = refs[6:6 + N_GROUPS]
        kv_refs = refs[6 + N_GROUPS:6 + 2 * N_GROUPS]
        scr = refs[6 + 2 * N_GROUPS:]
    else:
        x_ref, g_ref, w_ref, ct_ref, s1_ref, s2_ref, o_ref = refs
    hn = _rms(x_ref[...], g_ref[...]).astype(BF16)
    ct = ct_ref[...]
    s1 = s1_ref[...]
    s2 = s2_ref[...]
    per_group = ATT_WIDTH // LANES
    per_part = N_GROUPS * per_group
    for c in range(3 * per_part):
        cs = slice(c * LANES, (c + 1) * LANES)
        if c % 2 == 0:
            xb2 = _dot(hn, w_ref[:, c * LANES:(c + 2) * LANES])
        xb = xb2[:, (c % 2) * LANES:(c % 2 + 1) * LANES]
        part, rem = divmod(c, per_part)
        if part < 2:
            xb = (xb * ct + pltpu.roll(xb, LANES - ROT_DIM // 2, axis=1) * s1
                  + pltpu.roll(xb, ROT_DIM // 2, axis=1) * s2)
        if long_mode:
            gi, cb = divmod(rem, per_group)
            scr[gi][part * per_group + cb] = xb
        else:
            o_ref[:, cs] = xb
    if long_mode:
        for gi in range(N_GROUPS):
            window, dil = ATT_GROUPS[gi]
            kb = min(min(window, seq), tm)
            for cb in range(3 * per_group):
                cs = slice(cb * LANES, (cb + 1) * LANES)
                for res in range(dil):
                    a_refs[gi][0, res, :, cs] = scr[gi][cb, pl.ds(res, tm // dil, stride=dil), :]
                if cb >= per_group:
                    kv_refs[gi][0, :, (cb - per_group) * LANES:(cb - per_group + 1) * LANES] = (
                        scr[gi][cb, tm - kb:tm, :])


def _rope_tables(pos):
    half = ROT_DIM // 2
    inv = jnp.power(ROPE_THETA, -jnp.arange(half, dtype=F32) / half)
    ang = pos.astype(F32)[:, None] * inv[None, :]
    cos, sin = jnp.cos(ang), jnp.sin(ang)
    n = pos.shape[0]
    one = jnp.ones((n, ATT_HEAD_DIM - ROT_DIM), F32)
    zero = jnp.zeros((n, ATT_HEAD_DIM - ROT_DIM), F32)
    zh = jnp.zeros((n, half), F32)
    ct = jnp.concatenate([cos, cos, one], axis=1)
    s1 = jnp.concatenate([-sin, zh, zero], axis=1)
    s2 = jnp.concatenate([zh, sin, zero], axis=1)
    rep = LANES // ATT_HEAD_DIM
    return tuple(jnp.tile(a, (1, rep)) for a in (ct, s1, s2))


def _qkv_call(x, g, wqkv, tables, tm, batch=None):
    n, d = x.shape
    nq = wqkv.shape[1]
    gspec_shape = (1, d)
    if batch is None:
        row = pl.BlockSpec((tm, d), lambda i: (i, 0))
        tab = pl.BlockSpec((tm, LANES), lambda i: (0, 0))
        return pl.pallas_call(
            functools.partial(_qkv_kernel, long_mode=False, tm=tm, seq=None),
            grid=(n // tm,),
            in_specs=[row, pl.BlockSpec(gspec_shape, lambda i: (0, 0)),
                      pl.BlockSpec(wqkv.shape, lambda i: (0, 0)), tab, tab, tab],
            out_specs=pl.BlockSpec((tm, nq), lambda i: (i, 0)),
            out_shape=jax.ShapeDtypeStruct((n, nq), F32),
            compiler_params=_cparams("parallel"),
            name="attn_qkv_rope_short",
        )(x, g.reshape(1, d), wqkv, *tables)
    s = n // batch
    tpb = s // tm
    row = pl.BlockSpec((tm, d), lambda b, j: (b * tpb + j, 0))
    tab = pl.BlockSpec((tm, LANES), lambda b, j: (j, 0))
    a_specs, a_shapes, kv_specs, kv_shapes = [], [], [], []
    for window, dil in ATT_GROUPS:
        a_specs.append(pl.BlockSpec((1, dil, tm // dil, 3 * ATT_WIDTH), lambda b, j: (b, 0, j, 0)))
        a_shapes.append(jax.ShapeDtypeStruct((batch, dil, s // dil, 3 * ATT_WIDTH), F32))
        keep = min(window, s)
        kb = min(keep, tm)
        kv_specs.append(pl.BlockSpec(
            (1, kb, 2 * ATT_WIDTH),
            lambda b, j, kb=kb, keep=keep: (b, jnp.maximum((j + 1) * tm - kb - (s - keep), 0) // kb, 0)))
        kv_shapes.append(jax.ShapeDtypeStruct((batch, keep, 2 * ATT_WIDTH), F32))
    outs = pl.pallas_call(
        functools.partial(_qkv_kernel, long_mode=True, tm=tm, seq=s),
        grid=(batch, tpb),
        in_specs=[row, pl.BlockSpec(gspec_shape, lambda b, j: (0, 0)),
                  pl.BlockSpec(wqkv.shape, lambda b, j: (0, 0), pipeline_mode=pl.Buffered(1)),
                  tab, tab, tab],
        out_specs=a_specs + kv_specs,
        out_shape=a_shapes + kv_shapes,
        scratch_shapes=[pltpu.VMEM((3 * ATT_WIDTH // LANES, tm, LANES), F32)] * N_GROUPS,
        compiler_params=_cparams("parallel", "arbitrary"),
        name="attn_qkv_rope_long",
    )(x, g.reshape(1, d), wqkv, *tables)
    return outs[:N_GROUPS], outs[N_GROUPS:]


def _attn_prompt_kernel(q_ref, kp_ref, kc_ref, vp_ref, vc_ref, o_ref, l_ref, *, nr, nq):
    blk = ATT_BLK
    qi = lax.broadcasted_iota(jnp.int32, (blk, 2 * blk), 0)
    kj = lax.broadcasted_iota(jnp.int32, (blk, 2 * blk), 1)
    dist = qi + blk - kj
    band = (dist >= 0) & (dist <= blk)
    first = band & (pl.program_id(2) * nq * blk + kj - blk >= 0)
    low = lax.broadcasted_iota(jnp.int32, (blk, LANES), 1) < ATT_HEAD_DIM
    nt = (((1,), (1,)), ((), ()))
    for r in range(nr):
        for sub in range(nq):
            rows = slice(sub * blk, (sub + 1) * blk)
            prows = slice((sub - 1) * blk, sub * blk)
            valid = first if sub == 0 else band
            q = q_ref[0, r, rows, :] * ATT_SCALE
            for hp in range(ATT_WIDTH // LANES):
                cs = slice(hp * LANES, (hp + 1) * LANES)
                kprev = kp_ref[0, r, :, cs] if sub == 0 else kc_ref[0, r, prows, cs]
                vprev = vp_ref[0, r, :, cs] if sub == 0 else vc_ref[0, r, prows, cs]
                kp = jnp.concatenate([kprev, kc_ref[0, r, rows, cs]], axis=0).astype(BF16)
                vp = jnp.concatenate([vprev, vc_ref[0, r, rows, cs]], axis=0).astype(BF16)
                qp = q[:, cs]
                outs = []
                for par in range(2):
                    qm = jnp.where(low if par == 0 else jnp.logical_not(low), qp, 0.0).astype(BF16)
                    s = lax.dot_general(qm, kp, nt, preferred_element_type=F32)
                    s = jnp.where(valid, s, -jnp.inf)
                    m = jnp.max(s, axis=-1, keepdims=True)
                    p = jnp.exp(s - m)
                    l = jnp.sum(p, axis=-1, keepdims=True)
                    outs.append((_dot(p.astype(BF16), vp) / l,
                                 jnp.broadcast_to(m + jnp.log(l), (blk, LANES))))
                o_ref[0, r, rows, cs] = jnp.where(low, outs[0][0], outs[1][0])
                l_ref[0, r, rows, cs] = jnp.where(low, outs[0][1], outs[1][1])


def _attn_prompt_call(a, gi):
    b, dil, n_sub, _ = a.shape
    nb = n_sub // ATT_BLK
    nq = min(nb, ATT_STEP_BLOCKS)
    nr = min(dil, ATT_STEP_BLOCKS // nq)
    curshape = (1, nr, nq * ATT_BLK, ATT_WIDTH)

    def cur(part):
        return pl.BlockSpec(curshape, lambda i, r, n: (i, r, n, part))

    def prv(part):
        return pl.BlockSpec((1, nr, ATT_BLK, ATT_WIDTH),
                            lambda i, r, n: (i, r, jnp.maximum(n * nq - 1, 0), part))

    return pl.pallas_call(
        functools.partial(_attn_prompt_kernel, nr=nr, nq=nq),
        grid=(b, dil // nr, nb // nq),
        in_specs=[cur(0), prv(1), cur(1), prv(2), cur(2)],
        out_specs=[cur(0), cur(0)],
        out_shape=[jax.ShapeDtypeStruct((b, dil, n_sub, ATT_WIDTH), F32)] * 2,
        compiler_params=_cparams("parallel", "parallel", "parallel"),
        name=f"attn_prompt_g{gi}",
    )(a, a, a, a, a)


def _attn_sample_kernel(qkv_ref, c0_ref, c1_ref, c2_ref, o_ref, *, t):
    caches = (c0_ref, c1_ref, c2_ref)
    nrow = ATT_HEADS * t
    rowi = lax.broadcasted_iota(jnp.int32, (nrow, ATT_WIDTH), 0)
    coli = lax.broadcasted_iota(jnp.int32, (nrow, ATT_WIDTH), 1)
    headmask = (rowi // t) == (coli // ATT_HEAD_DIM)
    pad = jnp.zeros((LANES - t, ATT_WIDTH), F32)
    nt = (((1,), (1,)), ((), ()))
    stats = []
    for gi in range(N_GROUPS):
        window, dil = ATT_GROUPS[gi]
        cref = caches[gi]
        ln = cref.shape[-1]
        q = qkv_ref[0, :, gi * ATT_WIDTH:(gi + 1) * ATT_WIDTH] * ATT_SCALE
        knew = qkv_ref[0, :, (N_GROUPS + gi) * ATT_WIDTH:(N_GROUPS + gi + 1) * ATT_WIDTH]
        vnew = qkv_ref[0, :, (2 * N_GROUPS + gi) * ATT_WIDTH:(2 * N_GROUPS + gi + 1) * ATT_WIDTH]
        qexp = jnp.where(headmask, jnp.concatenate([q] * ATT_HEADS, axis=0), 0.0).astype(BF16)
        kn = jnp.concatenate([knew, pad], axis=0).astype(BF16)
        vn = jnp.concatenate([vnew, pad], axis=0).astype(BF16)
        sc = _dot(qexp, cref[0, 0].astype(BF16))
        sn = lax.dot_general(qexp, kn, nt, preferred_element_type=F32)
        qi_c = lax.broadcasted_iota(jnp.int32, (nrow, ln), 0) % t
        pos_c = lax.broadcasted_iota(jnp.int32, (nrow, ln), 1)
        delta_c = ln + qi_c - pos_c
        ok_c = (delta_c <= window) & (delta_c % dil == 0)
        qi_n = lax.broadcasted_iota(jnp.int32, (nrow, LANES), 0) % t
        j_n = lax.broadcasted_iota(jnp.int32, (nrow, LANES), 1)
        delta_n = qi_n - j_n
        ok_n = (j_n < t) & (delta_n >= 0) & (delta_n <= window) & (delta_n % dil == 0)
        sc = jnp.where(ok_c, sc, -jnp.inf)
        sn = jnp.where(ok_n, sn, -jnp.inf)
        m = jnp.maximum(jnp.max(sc, axis=-1, keepdims=True), jnp.max(sn, axis=-1, keepdims=True))
        pc = jnp.exp(sc - m)
        pn = jnp.exp(sn - m)
        l = jnp.sum(pc, axis=-1, keepdims=True) + jnp.sum(pn, axis=-1, keepdims=True)
        og = (lax.dot_general(pc.astype(BF16), cref[0, 1].astype(BF16), nt,
                              preferred_element_type=F32)
              + _dot(pn.astype(BF16), vn))
        stats.append((m + jnp.log(l), l, og))

    mx = functools.reduce(jnp.maximum, [s[0] for s in stats])
    es = [jnp.exp(s[0] - mx) for s in stats]
    den = es[0] + es[1] + es[2]
    full = sum(s[2] * (e / (den * s[1])) for s, e in zip(stats, es))
    for h in range(ATT_HEADS):
        hs = slice(h * ATT_HEAD_DIM, (h + 1) * ATT_HEAD_DIM)
        o_ref[0, :, hs] = full[h * t:(h + 1) * t, hs]


def _attn_sample_call(qkv, caches, b, t):
    nq = qkv.shape[-1]
    views = [c.transpose(0, 2, 3, 4, 1).reshape(b, 2, ATT_WIDTH, c.shape[1]) for c in caches]
    return pl.pallas_call(
        functools.partial(_attn_sample_kernel, t=t),
        grid=(b,),
        in_specs=[pl.BlockSpec((1, t, nq), lambda i: (i, 0, 0))]
                 + [pl.BlockSpec((1,) + v.shape[1:], lambda i: (i, 0, 0, 0)) for v in views],
        out_specs=pl.BlockSpec((1, t, ATT_WIDTH), lambda i: (i, 0, 0)),
        out_shape=jax.ShapeDtypeStruct((b, t, ATT_WIDTH), F32),
        compiler_params=_cparams("parallel"),
        name="attn_sample",
    )(qkv.reshape(b, t, nq), *views)


def _attn_out_kernel(*refs, merge, tm):
    if merge:
        o_refs = refs[:N_GROUPS]
        l_refs = refs[N_GROUPS:2 * N_GROUPS]
        x_ref, g_ref, wo_ref, out_ref = refs[2 * N_GROUPS:2 * N_GROUPS + 4]
        scr = refs[2 * N_GROUPS + 4:]
        vals = []
        for gi, ref in enumerate(o_refs + l_refs):
            dil = ATT_GROUPS[gi % N_GROUPS][1]
            if dil == 1:
                vals.append(ref[0, 0])
            else:
                for cb in range(ATT_WIDTH // LANES):
                    for res in range(dil):
                        scr[gi][cb, pl.ds(res, tm // dil, stride=dil), :] = (
                            ref[0, res, :, cb * LANES:(cb + 1) * LANES])
                vals.append(jnp.concatenate([scr[gi][cb] for cb in range(ATT_WIDTH // LANES)], axis=1))
        oa, ob, oc, la, lb, lc = vals
        mx = jnp.maximum(jnp.maximum(la, lb), lc)
        ea, eb, ec = jnp.exp(la - mx), jnp.exp(lb - mx), jnp.exp(lc - mx)
        den = ea + eb + ec
        o = (ea / den) * oa + (eb / den) * ob + (ec / den) * oc
    else:
        o0, x_ref, g_ref, wo_ref, out_ref = refs
        o = o0[...]
    y = _dot(o.astype(BF16), wo_ref[...])
    out_ref[...] = x_ref[...] + _rms(y, g_ref[...])


def _attn_out_call(os_, lses, x, g, wo, tm, batch=None):
    n, d = x.shape
    merge = lses is not None
    kern = functools.partial(_attn_out_kernel, merge=merge, tm=tm)
    if not merge:
        row = pl.BlockSpec((tm, d), lambda i: (i, 0))
        return pl.pallas_call(
            kern,
            grid=(n // tm,),
            in_specs=[pl.BlockSpec((tm, ATT_WIDTH), lambda i: (i, 0)), row,
                      pl.BlockSpec((1, d), lambda i: (0, 0)), pl.BlockSpec(wo.shape, lambda i: (0, 0))],
            out_specs=row,
            out_shape=jax.ShapeDtypeStruct((n, d), F32),
            compiler_params=_cparams("parallel"),
            name="attn_out_short",
        )(os_[0], x, g.reshape(1, d), wo)
    tpb = n // batch // tm
    row = pl.BlockSpec((tm, d), lambda b, j: (b * tpb + j, 0))
    aspecs = [pl.BlockSpec((1, dil, tm // dil, ATT_WIDTH), lambda b, j: (b, 0, j, 0))
              for _, dil in ATT_GROUPS]
    return pl.pallas_call(
        kern,
        grid=(batch, tpb),
        in_specs=aspecs * 2 + [row, pl.BlockSpec((1, d), lambda b, j: (0, 0)),
                               pl.BlockSpec(wo.shape, lambda b, j: (0, 0))],
        out_specs=row,
        out_shape=jax.ShapeDtypeStruct((n, d), F32),
        scratch_shapes=[pltpu.VMEM((ATT_WIDTH // LANES, tm, LANES), F32)] * (2 * N_GROUPS),
        compiler_params=_cparams("parallel", "parallel"),
        name="attn_out_long",
    )(*os_, *lses, x, g.reshape(1, d), wo)


def _pad_rows8(a):
    return jnp.concatenate([a, jnp.zeros((SUBLANES - a.shape[0],) + a.shape[1:], a.dtype)], axis=0)


def _pad_to(a, axis, size):
    padw = [(0, 0)] * a.ndim
    padw[axis] = (0, size - a.shape[axis])
    return jnp.pad(a, padw)


def _prep_weights(norm_g, rw, at_wqkv, at_wo, ff):
    (mu, wr, wk, wv, wo, w0, w1, w2, a0, a1, a2, g1, g2, k_k, k_a, r_k, lnx_g, lnx_b) = [p[0] for p in rw]
    glora = 2 * LANES
    sc = lambda a: _to_scan_heads(a.astype(BF16))
    proj_ws = [sc(wr), sc(wk), sc(wv),
               _pad_to(w1, 1, LANES).astype(BF16), _pad_to(sc(w2), 0, LANES),
               _pad_to(a1, 1, LANES).astype(BF16), _pad_to(sc(a2), 0, LANES),
               _pad_to(g1, 1, glora).astype(BF16), _pad_to(g2, 0, glora).astype(BF16)]
    return dict(
        mu8=_pad_rows8(mu),
        proj_vec=_pad_rows8(_to_scan_heads(jnp.stack([w0, a0, k_k, k_a, r_k.reshape(-1)]))),
        proj_ws=proj_ws,
        out_vec=_pad_rows8(jnp.stack([lnx_g, lnx_b, norm_g[0, 1]])),
        rw_wo=wo.astype(BF16),
        wqkv=at_wqkv[0].astype(BF16),
        at_wo=at_wo[0].astype(BF16),
        ff=[(_pad_rows8(jnp.stack([norm_g[l, 2], norm_g[l, 3]])), ff[0][l].astype(BF16),
             _pad_rows8(ff[1][l]), _pad_rows8(ff[2][l][None]), ff[3][l].astype(BF16))
            for l in range(2)],
        ones=_seg_ones(),
    )


def _run_group(x, pos, shift0, wkv0, kv_bufs, conv0, norm_g, w, tm):
    b, t, d = x.shape
    n = b * t
    tm = min(tm, n)
    long_mode = shift0 is None
    xf = x.reshape(n, d)
    gn0 = norm_g[0, 0].reshape(1, d)
    to3 = lambda a: a.reshape(b, t, d)

    if long_mode:
        r, dec, k, kk, ka, g, bonus, v, hl = _rwkv_proj_call(xf, gn0, w, min(tm, 256), batch=b)
        new_shift = hl[:, SUBLANES - 1][None]
        s0 = jnp.zeros((b, SCAN_ROWS, LANES), F32)
    else:
        sh = jnp.concatenate([shift0[0][:, None, :], jnp.zeros((b, t - 1, d), F32)], axis=1)
        r, dec, k, kk, ka, g, bonus, v, hn = _rwkv_proj_call(
            xf, gn0, w, tm, seq_len=t, shift_rows=sh.reshape(n, d))
        new_shift = to3(hn)[:, -1][None]
        s0 = _state_to_scan(wkv0[0])
    y, s_fin = _scan_call(to3(kk), to3(dec), to3(k), to3(ka), to3(r), to3(v), s0,
                          min(t, SCAN_CHUNK))
    x1 = _rwkv_out_call(y.reshape(n, d), bonus, g, xf, w["out_vec"], w["rw_wo"], w["ones"], tm)
    new_wkv = _state_from_scan(s_fin)[None]

    def ffn(xin, layer):
        g2, wup, cw8, cb8, wdown = w["ff"][layer]
        if long_mode:
            ftm = min(t, FF_LONG_TILE)
            xo, hl = _ffn_call(xin, g2, wup, cw8, cb8, wdown, ftm, batch=b)
            cs = hl.reshape(b, t // ftm, SUBLANES, 2 * D_FF)[:, -1, SUBLANES - (CONV_W - 1):]
        else:
            st = conv0[layer]
            z = jnp.zeros((b, t - 1, 2 * D_FF), F32)
            p1 = jnp.concatenate([st[:, 1:2], z], axis=1).reshape(n, 2 * D_FF)
            p2 = jnp.concatenate([st, z[:, 1:]], axis=1).reshape(n, 2 * D_FF)
            xo, hl = _ffn_call(xin, g2, wup, cw8, cb8, wdown, tm, seq_len=t, prev=(p1, p2))
            cs = hl.reshape(b, t, 2 * D_FF)[:, t - (CONV_W - 1):]
        return xo, cs

    x2, conv_a = ffn(x1, 0)

    new_kv = []
    if long_mode:
        a_g, kv_g = _qkv_call(x2, norm_g[1, 0], w["wqkv"], _rope_tables(pos), tm, batch=b)
        for kv in kv_g:
            new_kv.append(kv.reshape(1, b, kv.shape[1], 2, ATT_HEADS, ATT_HEAD_DIM))
        res = [_attn_prompt_call(a_g[gi], gi) for gi in range(N_GROUPS)]
        x3 = _attn_out_call([o for o, _ in res], [l for _, l in res], x2, norm_g[1, 1],
                            w["at_wo"], tm, batch=b)
    else:
        tables = tuple(jnp.tile(a, (tm // t, 1)) for a in _rope_tables(pos))
        qkv = _qkv_call(x2, norm_g[1, 0], w["wqkv"], tables, tm)
        qkv3 = qkv.reshape(b, t, 3 * N_GROUPS, ATT_HEADS, ATT_HEAD_DIM)
        for gi in range(N_GROUPS):
            new_kv.append(jnp.stack([qkv3[:, :, N_GROUPS + gi], qkv3[:, :, 2 * N_GROUPS + gi]],
                                    axis=2)[None])
        o = _attn_sample_call(qkv, [buf[0] for buf in kv_bufs], b, t)
        x3 = _attn_out_call([o.reshape(n, ATT_WIDTH)], None, x2, norm_g[1, 1], w["at_wo"], tm)
    x4, conv_b = ffn(x3, 1)
    return (x4.reshape(b, t, d), new_shift, new_wkv, new_kv, jnp.stack([conv_a, conv_b]))


def kernel(x_prompt, x_sample, state_shift, state_wkv, cache_kv_w128, cache_kv_w512, cache_kv_w2048, state_conv, norm_g, rw_mu, rw_wr, rw_wk, rw_wv, rw_wo, rw_w0, rw_w1, rw_w2, rw_a0, rw_a1, rw_a2, rw_g1, rw_g2, rw_kk, rw_ka, rw_rk, rw_lnx_g, rw_lnx_b, at_wqkv, at_wo, ff_wup, ff_conv_w, ff_conv_b, ff_wdown):
    rw = (rw_mu, rw_wr, rw_wk, rw_wv, rw_wo, rw_w0, rw_w1, rw_w2, rw_a0, rw_a1, rw_a2,
          rw_g1, rw_g2, rw_kk, rw_ka, rw_rk, rw_lnx_g, rw_lnx_b)
    ff = (ff_wup, ff_conv_w, ff_conv_b, ff_wdown)
    w = _prep_weights(norm_g, rw, at_wqkv, at_wo, ff)
    tp = x_prompt.shape[1]
    ts = x_sample.shape[1]
    pos_p = jnp.arange(tp, dtype=jnp.int32)
    y_p, p_shift, p_wkv, p_kv, p_conv = _run_group(
        x_prompt, pos_p, None, None, None, None, norm_g, w, 512)
    pos_s = PAST_LEN + jnp.arange(ts, dtype=jnp.int32)
    y_s, s_shift, s_wkv, s_kv, s_conv = _run_group(
        x_sample, pos_s, state_shift, state_wkv,
        (cache_kv_w128, cache_kv_w512, cache_kv_w2048), state_conv, norm_g, w, 256)
    return (y_p, y_s, p_shift, p_wkv, p_kv[0], p_kv[1], p_kv[2], p_conv,
            s_shift, s_wkv, s_kv[0], s_kv[1], s_kv[2], s_conv)
```

```python
import functools
import math

import jax
import jax.numpy as jnp
import numpy as np
from jax import lax
from jax.experimental import pallas as pl
from jax.experimental.pallas import tpu as pltpu

F32 = jnp.float32
BF16 = jnp.bfloat16

D_MODEL = 1024
RW_HEAD = 64
RW_HEADS = D_MODEL // RW_HEAD
RW_GN_EPS = 64e-5
ATT_GROUPS = ((128, 1), (512, 4), (2048, 16))
N_GROUPS = 3
ATT_HEADS = 8
ATT_HEAD_DIM = 64
ATT_WIDTH = ATT_HEADS * ATT_HEAD_DIM
ATT_SCALE = ATT_HEAD_DIM ** -0.5
ATT_BLK = 128
ATT_STEP_BLOCKS = 4
ROT_DIM = ATT_HEAD_DIM // 4
ROPE_THETA = 500000.0
D_FF = 2816
CONV_W = 3
NORM_EPS = 1e-6
PAST_LEN = 2048

LANES = 128
SUBLANES = 8
FF_CHUNK = 256
FF_DOWN_CHUNKS = 11
FF_LONG_TILE = 512
SCAN_CHUNK = 128
SCAN_HALF = 64
SCAN_ROWS = RW_HEADS // 2 * RW_HEAD
SCAN_NB = 2
SCAN_NB_SHORT = 4
VMEM_LIMIT = 56 * 1024 * 1024


def _cparams(*sem):
    return pltpu.CompilerParams(dimension_semantics=sem, vmem_limit_bytes=VMEM_LIMIT)


def _dot(a, b):
    return jnp.dot(a, b, preferred_element_type=F32)


def _rms(x, g):
    ms = jnp.mean(x * x, axis=-1, keepdims=True)
    return x * lax.rsqrt(ms + NORM_EPS) * g


def _segsum(x, ones):
    outs = []
    for c in range(x.shape[1] // LANES):
        xc = x[:, c * LANES:(c + 1) * LANES]
        hi = xc.astype(BF16)
        lo = (xc - hi.astype(F32)).astype(BF16)
        outs.append(_dot(hi, ones) + _dot(lo, ones))
    return jnp.concatenate(outs, axis=1)


def _scan_to_natural_heads(x):
    low = lax.broadcasted_iota(jnp.int32, (x.shape[0], LANES), 1) < RW_HEAD
    outs = []
    for c in range(0, x.shape[1] // LANES, 2):
        a = x[:, c * LANES:(c + 1) * LANES]
        b = x[:, (c + 1) * LANES:(c + 2) * LANES]
        outs.append(jnp.where(low, a, pltpu.roll(b, RW_HEAD, axis=1)))
        outs.append(jnp.where(low, pltpu.roll(a, RW_HEAD, axis=1), b))
    return jnp.concatenate(outs, axis=1)


def _rwkv_proj_kernel(*refs, seq_len, tm):
    if seq_len is None:
        (x_ref, gn_ref, mu_ref, vec_ref, wr_ref, wk_ref, wv_ref, w1_ref, w2_ref, a1_ref,
         a2_ref, g1_ref, g2_ref, ones_ref,
         r_ref, d_ref, k_ref, kk_ref, ka_ref, g_ref, bonus_ref, v_ref, hl_ref, carry) = refs
    else:
        (x_ref, gn_ref, mu_ref, vec_ref, wr_ref, wk_ref, wv_ref, w1_ref, w2_ref, a1_ref,
         a2_ref, g1_ref, g2_ref, ones_ref, sh_ref,
         r_ref, d_ref, k_ref, kk_ref, ka_ref, g_ref, bonus_ref, v_ref, hn_ref) = refs
    hn = _rms(x_ref[...], gn_ref[...])
    rolled = pltpu.roll(hn, 1, axis=0)
    rows = lax.broadcasted_iota(jnp.int32, (tm, D_MODEL), 0)
    if seq_len is None:
        @pl.when(pl.program_id(1) == 0)
        def _():
            carry[...] = jnp.zeros(carry.shape, F32)

        prev = jnp.where(rows == 0, carry[0:1, :], rolled)
        carry[0:1, :] = hn[tm - 1:tm, :]
        hl_ref[0] = hn[tm - SUBLANES:tm, :]
    else:
        prev = jnp.where(rows % seq_len == 0, sh_ref[...], rolled)
        hn_ref[...] = hn
    xx = prev - hn

    def mix(i):
        return (hn + xx * mu_ref[i:i + 1, :]).astype(BF16)

    w0 = vec_ref[0:1, :]
    a0 = vec_ref[1:2, :]
    k_k = vec_ref[2:3, :]
    k_a = vec_ref[3:4, :]
    r_k = vec_ref[4:5, :]

    r = _dot(mix(0), wr_ref[...])
    k = _dot(mix(2), wk_ref[...])
    v = _dot(mix(3), wv_ref[...])
    wl = _dot(jnp.tanh(_dot(mix(1), w1_ref[...])).astype(BF16), w2_ref[...])
    z = -(w0 + wl)
    softplus = jnp.maximum(z, 0.0) + jnp.log1p(jnp.exp(-jnp.abs(z)))
    w = -softplus - 0.5
    decay = jnp.exp(-jnp.exp(w))
    al = _dot(_dot(mix(4), a1_ref[...]).astype(BF16), a2_ref[...])
    a = jax.nn.sigmoid(a0 + al)
    g = _dot(jax.nn.sigmoid(_dot(mix(5), g1_ref[...])).astype(BF16), g2_ref[...])

    ones = ones_ref[...]
    kk = k * k_k
    nrm = jnp.sqrt(_segsum(kk * kk, ones))
    kk = kk / jnp.maximum(nrm, 1e-12)
    k2 = k * (1.0 + (a - 1.0) * k_a)

    r_ref[...] = r
    d_ref[...] = decay
    k_ref[...] = k2
    kk_ref[...] = kk
    ka_ref[...] = kk * a
    g_ref[...] = g.astype(BF16)
    bonus_ref[...] = _scan_to_natural_heads(_segsum(r * k2 * r_k, ones) * v).astype(BF16)
    v_ref[...] = v


def _rwkv_proj_call(x, gn, w, tm, batch=None, seq_len=None, shift_rows=None):
    n, d = x.shape
    nd = [jax.ShapeDtypeStruct((n, d), F32)]
    ndh = [jax.ShapeDtypeStruct((n, d), BF16)]
    ws = w["proj_ws"]
    if seq_len is None:
        t = n // batch
        tpb = t // tm
        row = pl.BlockSpec((tm, d), lambda b, j: (b * tpb + j, 0))

        def full(a):
            return pl.BlockSpec(a.shape, lambda b, j: (0,) * a.ndim, pipeline_mode=pl.Buffered(1))

        ins = [x, gn, w["mu8"], w["proj_vec"]] + ws + [w["ones"]]
        return pl.pallas_call(
            functools.partial(_rwkv_proj_kernel, seq_len=None, tm=tm),
            grid=(batch, tpb),
            in_specs=[row] + [full(a) for a in ins[1:]],
            out_specs=[row] * 8 + [pl.BlockSpec((1, SUBLANES, d), lambda b, j: (b, 0, 0))],
            out_shape=nd * 5 + ndh * 2 + nd + [jax.ShapeDtypeStruct((batch, SUBLANES, d), F32)],
            scratch_shapes=[pltpu.VMEM((SUBLANES, d), F32)],
            compiler_params=_cparams("parallel", "arbitrary"),
            name="rwkv_proj_long",
        )(*ins)
    row = pl.BlockSpec((tm, d), lambda i: (i, 0))

    def full(a):
        return pl.BlockSpec(a.shape, lambda i: (0,) * a.ndim)

    ins = [x, gn, w["mu8"], w["proj_vec"]] + ws + [w["ones"]]
    return pl.pallas_call(
        functools.partial(_rwkv_proj_kernel, seq_len=seq_len, tm=tm),
        grid=(n // tm,),
        in_specs=[row] + [full(a) for a in ins[1:]] + [row],
        out_specs=[row] * 9,
        out_shape=nd * 5 + ndh * 2 + nd * 2,
        compiler_params=_cparams("parallel"),
        name="rwkv_proj_short",
    )(*ins, shift_rows)


def _scan_kernel(kk_ref, d_ref, k_ref, ka_ref, r_ref, v_ref, s0_ref, selv_ref, place_ref,
                 ones_ref, y_ref, sout_ref, s_scr, lv_scr, yt_scr, *, steps, nb):
    c = pl.program_id(1)

    @pl.when(c == 0)
    def _():
        s_scr[...] = s0_ref[...]

    ones = ones_ref[...]
    low = lax.broadcasted_iota(jnp.int32, (SCAN_ROWS, LANES), 1) < RW_HEAD
    n_half = -(-steps // SCAN_HALF)
    for b in range(nb):
        vb = v_ref[b]
        if steps < LANES:
            vb = jnp.concatenate([vb, jnp.zeros((LANES - steps, D_MODEL), F32)], axis=0)
        tr = [vb[:, hp * LANES:(hp + 1) * LANES].T for hp in range(RW_HEADS // 2)]
        ev = jnp.concatenate([x[0:RW_HEAD] for x in tr], axis=0)
        od = jnp.concatenate([x[RW_HEAD:2 * RW_HEAD] for x in tr], axis=0)
        lv_scr[b, 0] = jnp.where(low, ev, pltpu.roll(od, RW_HEAD, axis=1)).astype(BF16)
        if n_half > 1:
            lv_scr[b, 1] = jnp.where(low, pltpu.roll(ev, RW_HEAD, axis=1), od).astype(BF16)

    def rows(blk, j):
        return jnp.concatenate(
            [jnp.broadcast_to(blk[j:j + 1, hp * LANES:(hp + 1) * LANES], (RW_HEAD, LANES))
             for hp in range(RW_HEADS // 2)], axis=0)

    for half in range(n_half):
        nst = min(SCAN_HALF, steps - half * SCAN_HALF)
        yt_scr[...] = jnp.zeros(yt_scr.shape, F32)

        def body(grp, carry, half=half):
            base = pl.multiple_of(half * SCAN_HALF + grp * SUBLANES, SUBLANES)
            blks = [[ref[b, pl.ds(base, SUBLANES), :]
                     for ref in (kk_ref, d_ref, k_ref, ka_ref, r_ref)] for b in range(nb)]
            p2 = [[] for _ in range(nb)]
            for j in range(SUBLANES):
                for b in range(nb):
                    kk8, d8, k8, ka8, r8 = blks[b]
                    s = s_scr[b]
                    sk = _dot((s * rows(kk8, j)).astype(BF16), ones)
                    vcol = _dot(lv_scr[b, half], selv_ref[grp * SUBLANES + j])
                    s = s * rows(d8, j) - sk * rows(ka8, j) + vcol * rows(k8, j)
                    s_scr[b] = s
                    p2[b].append((s * rows(r8, j)).astype(BF16))
            prow = pl.multiple_of(grp * SUBLANES * LANES, SUBLANES * LANES)
            pl8 = place_ref[pl.ds(prow, SUBLANES * LANES), :]
            for b in range(nb):
                yt_scr[b] += _dot(jnp.concatenate(p2[b], axis=1), pl8)
            return carry

        lax.fori_loop(0, nst // SUBLANES, body, 0)

        t0 = half * SCAN_HALF
        for b in range(nb):
            yt = yt_scr[b]
            for q in range(RW_HEADS // 4):
                tr = yt[q * 2 * RW_HEAD:(q + 1) * 2 * RW_HEAD, :].T
                c0 = q * 2 * LANES
                y_ref[b, t0:t0 + nst, c0:c0 + LANES] = tr[0:nst]
                y_ref[b, t0:t0 + nst, c0 + LANES:c0 + 2 * LANES] = tr[SCAN_HALF:SCAN_HALF + nst]

    @pl.when(c == pl.num_programs(1) - 1)
    def _():
        sout_ref[...] = s_scr[...]


def _scan_tables():
    col = np.arange(LANES)
    row = np.arange(LANES)
    t = np.arange(SCAN_HALF)
    selv = ((row[None, :, None] // RW_HEAD == col[None, None, :] // RW_HEAD)
            & (row[None, :, None] % RW_HEAD == t[:, None, None]))
    place = ((row[None, :, None] // RW_HEAD == col[None, None, :] // RW_HEAD)
             & (col[None, None, :] % RW_HEAD == t[:, None, None]))
    return (jnp.asarray(selv, BF16),
            jnp.asarray(place.reshape(SCAN_HALF * LANES, LANES), BF16))


def _seg_ones():
    i = np.arange(LANES)
    return jnp.asarray(i[:, None] // RW_HEAD == i[None, :] // RW_HEAD, BF16)


def _scan_call(kk, dec, k, ka, r, v, s0, steps):
    b, t, d = kk.shape
    assert steps % SUBLANES == 0 and t % steps == 0
    nb = SCAN_NB if steps == SCAN_CHUNK else SCAN_NB_SHORT
    nb = nb if b % nb == 0 else 1
    nc = t // steps
    n_half = -(-steps // SCAN_HALF)
    selv, place = _scan_tables()
    ones = _seg_ones()
    rowspec = pl.BlockSpec((nb, steps, d), lambda i, c: (i, c, 0))

    def full(a):
        return pl.BlockSpec(a.shape, lambda i, c: (0,) * a.ndim)

    sspec = pl.BlockSpec((nb, SCAN_ROWS, LANES), lambda i, c: (i, 0, 0))
    return pl.pallas_call(
        functools.partial(_scan_kernel, steps=steps, nb=nb),
        grid=(b // nb, nc),
        in_specs=[rowspec] * 6 + [sspec, full(selv), full(place), full(ones)],
        out_specs=[rowspec, sspec],
        out_shape=[jax.ShapeDtypeStruct((b, t, d), F32),
                   jax.ShapeDtypeStruct((b, SCAN_ROWS, LANES), F32)],
        scratch_shapes=[pltpu.VMEM((nb, SCAN_ROWS, LANES), F32),
                        pltpu.VMEM((nb, n_half, SCAN_ROWS, LANES), BF16),
                        pltpu.VMEM((nb, SCAN_ROWS, LANES), F32)],
        compiler_params=_cparams("parallel", "arbitrary"),
        name="rwkv_scan",
    )(kk, dec, k, ka, r, v, s0, selv, place, ones)


def _to_scan_heads(a):
    lead = a.shape[:-1]
    return a.reshape(lead + (RW_HEADS // 4, 2, 2, RW_HEAD)).swapaxes(-3, -2).reshape(
        lead + (D_MODEL,))


def _state_to_scan(s):
    b = s.shape[0]
    s = s.reshape(b, RW_HEADS // 4, 2, 2, RW_HEAD, RW_HEAD)
    return s.transpose(0, 1, 3, 4, 2, 5).reshape(b, SCAN_ROWS, LANES)


def _state_from_scan(s):
    b = s.shape[0]
    s = s.reshape(b, RW_HEADS // 4, 2, RW_HEAD, 2, RW_HEAD)
    return s.transpose(0, 1, 4, 2, 3, 5).reshape(b, RW_HEADS, RW_HEAD, RW_HEAD)


def _rwkv_out_kernel(y_ref, bonus_ref, g_ref, x_ref, vec_ref, wo_ref, ones_ref, o_ref):
    ones = ones_ref[...]
    y = y_ref[...]
    mean = _segsum(y, ones) * (1.0 / RW_HEAD)
    yc = y - mean
    var = _segsum(yc * yc, ones) * (1.0 / RW_HEAD)
    yn = yc * lax.rsqrt(var + RW_GN_EPS) * vec_ref[0:1, :] + vec_ref[1:2, :]
    z = ((yn + bonus_ref[...]) * g_ref[...]).astype(BF16)
    mo = _dot(z, wo_ref[...])
    o_ref[...] = x_ref[...] + _rms(mo, vec_ref[2:3, :])


def _rwkv_out_call(y, bonus, g, x, vec8, wo, ones, tm):
    n, d = y.shape
    row = pl.BlockSpec((tm, d), lambda i: (i, 0))

    def full(a):
        return pl.BlockSpec(a.shape, lambda i: (0,) * a.ndim)

    return pl.pallas_call(
        _rwkv_out_kernel,
        grid=(n // tm,),
        in_specs=[row] * 4 + [full(vec8), full(wo), full(ones)],
        out_specs=row,
        out_shape=jax.ShapeDtypeStruct((n, d), F32),
        compiler_params=_cparams("parallel"),
        name="rwkv_out",
    )(y, bonus, g, x, vec8, wo, ones)


def _gelu_tanh(x):
    return 0.5 * x * (1.0 + jnp.tanh(math.sqrt(2.0 / math.pi) * (x + 0.044715 * (x * x * x))))


def _ffn_kernel(*refs, seq_len, tm):
    if seq_len is None:
        (x_ref, g_ref, wup_ref, cw_ref, cb_ref, wdown_ref, xo_ref, hl_ref,
         carry, hs_scr, act_scr) = refs
    else:
        x_ref, g_ref, wup_ref, cw_ref, cb_ref, wdown_ref, p1_ref, p2_ref, xo_ref, hl_ref = refs
    x = x_ref[...]
    hn = _rms(x, g_ref[0:1, :]).astype(BF16)
    n_chunks = D_FF // FF_CHUNK
    if seq_len is None:
        @pl.when(pl.program_id(1) == 0)
        def _():
            carry[...] = jnp.zeros(carry.shape, F32)
    else:
        rows = lax.broadcasted_iota(jnp.int32, (tm, FF_CHUNK), 0)
        tpos = rows % seq_len

    acc = jnp.zeros((tm, D_MODEL), F32)
    seg0 = 0
    for j in range(n_chunks):
        conv = []
        for part in range(2):
            c0 = part * D_FF + j * FF_CHUNK
            cs = slice(c0, c0 + FF_CHUNK)
            h = _dot(hn, wup_ref[:, cs])
            if seq_len is None:
                buf = (j % 2) * 2 + part
                hs_scr[buf, 0:SUBLANES, :] = carry[:, cs]
                hs_scr[buf, SUBLANES:SUBLANES + tm, :] = h
                carry[:, cs] = h[tm - SUBLANES:tm, :]
                hl_ref[:, cs] = h[tm - SUBLANES:tm, :]
                h1 = hs_scr[buf, SUBLANES - 1:SUBLANES - 1 + tm, :]
                h2 = hs_scr[buf, SUBLANES - 2:SUBLANES - 2 + tm, :]
            else:
                h1 = jnp.where(tpos == 0, p1_ref[:, cs], pltpu.roll(h, 1, axis=0))
                h2 = jnp.where(tpos < 2, p2_ref[:, cs], pltpu.roll(h, 2, axis=0))
                hl_ref[:, cs] = h
            conv.append(cb_ref[0:1, cs] + h2 * cw_ref[0:1, cs] + h1 * cw_ref[1:2, cs]
                        + h * cw_ref[2:3, cs])
        act = (_gelu_tanh(conv[0]) * conv[1]).astype(BF16)
        if seq_len is None:
            act_scr[:, j * FF_CHUNK:(j + 1) * FF_CHUNK] = act
            if (j + 1) % FF_DOWN_CHUNKS == 0 or j == n_chunks - 1:
                seg = slice(seg0 * FF_CHUNK, (j + 1) * FF_CHUNK)
                acc = acc + _dot(act_scr[:, seg], wdown_ref[seg, :])
                seg0 = j + 1
        else:
            acc = acc + _dot(act, wdown_ref[j * FF_CHUNK:(j + 1) * FF_CHUNK, :])
    xo_ref[...] = x + _rms(acc, g_ref[1:2, :])


def _ffn_call(x, g2, wup, cw8, cb8, wdown, tm, seq_len=None, prev=None, batch=None):
    n, d = x.shape
    f2 = 2 * D_FF
    kern = functools.partial(_ffn_kernel, seq_len=seq_len, tm=tm)
    if seq_len is None:
        tpb = n // batch // tm
        row = pl.BlockSpec((tm, d), lambda b, j: (b * tpb + j, 0))

        def full(a):
            return pl.BlockSpec(a.shape, lambda b, j: (0,) * a.ndim, pipeline_mode=pl.Buffered(1))

        return pl.pallas_call(
            kern,
            grid=(batch, tpb),
            in_specs=[row, full(g2), full(wup), full(cw8), full(cb8), full(wdown)],
            out_specs=[row, pl.BlockSpec((SUBLANES, f2), lambda b, j: (b * tpb + j, 0))],
            out_shape=[jax.ShapeDtypeStruct((n, d), F32),
                       jax.ShapeDtypeStruct((n // tm * SUBLANES, f2), F32)],
            scratch_shapes=[pltpu.VMEM((SUBLANES, f2), F32),
                            pltpu.VMEM((4, tm + SUBLANES, FF_CHUNK), F32),
                            pltpu.VMEM((tm, D_FF), BF16)],
            compiler_params=_cparams("parallel", "arbitrary"),
            name="conv_ffn_long",
        )(x, g2, wup, cw8, cb8, wdown)
    p1, p2 = prev
    row = pl.BlockSpec((tm, d), lambda i: (i, 0))
    hrow = pl.BlockSpec((tm, f2), lambda i: (i, 0))

    def full(a):
        return pl.BlockSpec(a.shape, lambda i: (0,) * a.ndim)

    return pl.pallas_call(
        kern,
        grid=(n // tm,),
        in_specs=[row, full(g2), full(wup), full(cw8), full(cb8), full(wdown), hrow, hrow],
        out_specs=[row, hrow],
        out_shape=[jax.ShapeDtypeStruct((n, d), F32), jax.ShapeDtypeStruct((n, f2), F32)],
        compiler_params=_cparams("parallel"),
        name="conv_ffn_short",
    )(x, g2, wup, cw8, cb8, wdown, p1, p2)


def _qkv_kernel(*refs, long_mode, tm, seq):
    if long_mode:
        x_ref, g_ref, w_ref, ct_ref, s1_ref, s2_ref = refs[:6]
        a_refs = refs[6:6 + N_GROUPS]
        kv_refs = refs[6 + N_GROUPS:6 + 2 * N_GROUPS]
        scr = refs[6 + 2 * N_GROUPS:]
    else:
        x_ref, g_ref, w_ref, ct_ref, s1_ref, s2_ref, o_ref = refs
    hn = _rms(x_ref[...], g_ref[...]).astype(BF16)
    ct = ct_ref[...]
    s1 = s1_ref[...]
    s2 = s2_ref[...]
    per_group = ATT_WIDTH // LANES
    per_part = N_GROUPS * per_group
    for c in range(3 * per_part):
        cs = slice(c * LANES, (c + 1) * LANES)
        if c % 2 == 0:
            xb2 = _dot(hn, w_ref[:, c * LANES:(c + 2) * LANES])
        xb = xb2[:, (c % 2) * LANES:(c % 2 + 1) * LANES]
        part, rem = divmod(c, per_part)
        if part < 2:
            xb = (xb * ct + pltpu.roll(xb, LANES - ROT_DIM // 2, axis=1) * s1
                  + pltpu.roll(xb, ROT_DIM // 2, axis=1) * s2)
        if long_mode:
            gi, cb = divmod(rem, per_group)
            scr[gi][part * per_group + cb] = xb * ATT_SCALE if part == 0 else xb
        else:
            o_ref[:, cs] = xb
    if long_mode:
        for gi in range(N_GROUPS):
            window, dil = ATT_GROUPS[gi]
            kb = min(min(window, seq), tm)
            for cb in range(3 * per_group):
                cs = slice(cb * LANES, (cb + 1) * LANES)
                for res in range(dil):
                    a_refs[gi][0, res, :, cs] = (
                        scr[gi][cb, pl.ds(res, tm // dil, stride=dil), :].astype(BF16))
                if cb >= per_group:
                    kv_refs[gi][0, :, (cb - per_group) * LANES:(cb - per_group + 1) * LANES] = (
                        scr[gi][cb, tm - kb:tm, :])


def _rope_tables(pos):
    half = ROT_DIM // 2
    inv = np.power(ROPE_THETA, -np.arange(half, dtype=np.float64) / half)
    ang = np.asarray(pos, np.float64)[:, None] * inv[None, :]
    cos, sin = np.cos(ang), np.sin(ang)
    n = len(pos)
    one = np.ones((n, ATT_HEAD_DIM - ROT_DIM))
    zero = np.zeros((n, ATT_HEAD_DIM - ROT_DIM))
    zh = np.zeros((n, half))
    ct = np.concatenate([cos, cos, one], axis=1)
    s1 = np.concatenate([-sin, zh, zero], axis=1)
    s2 = np.concatenate([zh, sin, zero], axis=1)
    rep = LANES // ATT_HEAD_DIM
    return tuple(np.tile(a, (1, rep)).astype(np.float32) for a in (ct, s1, s2))


def _qkv_call(x, g, wqkv, tables, tm, batch=None):
    n, d = x.shape
    nq = wqkv.shape[1]
    gspec_shape = (1, d)
    if batch is None:
        row = pl.BlockSpec((tm, d), lambda i: (i, 0))
        tab = pl.BlockSpec((tm, LANES), lambda i: (0, 0))
        return pl.pallas_call(
            functools.partial(_qkv_kernel, long_mode=False, tm=tm, seq=None),
            grid=(n // tm,),
            in_specs=[row, pl.BlockSpec(gspec_shape, lambda i: (0, 0)),
                      pl.BlockSpec(wqkv.shape, lambda i: (0, 0)), tab, tab, tab],
            out_specs=pl.BlockSpec((tm, nq), lambda i: (i, 0)),
            out_shape=jax.ShapeDtypeStruct((n, nq), F32),
            compiler_params=_cparams("parallel"),
            name="attn_qkv_rope_short",
        )(x, g.reshape(1, d), wqkv, *tables)
    s = n // batch
    tpb = s // tm
    row = pl.BlockSpec((tm, d), lambda b, j: (b * tpb + j, 0))
    tab = pl.BlockSpec((tm, LANES), lambda b, j: (j, 0))
    a_specs, a_shapes, kv_specs, kv_shapes = [], [], [], []
    for window, dil in ATT_GROUPS:
        a_specs.append(pl.BlockSpec((1, dil, tm // dil, 3 * ATT_WIDTH), lambda b, j: (b, 0, j, 0)))
        a_shapes.append(jax.ShapeDtypeStruct((batch, dil, s // dil, 3 * ATT_WIDTH), BF16))
        keep = min(window, s)
        kb = min(keep, tm)
        kv_specs.append(pl.BlockSpec(
            (1, kb, 2 * ATT_WIDTH),
            lambda b, j, kb=kb, keep=keep: (b, jnp.maximum((j + 1) * tm - kb - (s - keep), 0) // kb, 0)))
        kv_shapes.append(jax.ShapeDtypeStruct((batch, keep, 2 * ATT_WIDTH), F32))
    outs = pl.pallas_call(
        functools.partial(_qkv_kernel, long_mode=True, tm=tm, seq=s),
        grid=(batch, tpb),
        in_specs=[row, pl.BlockSpec(gspec_shape, lambda b, j: (0, 0)),
                  pl.BlockSpec(wqkv.shape, lambda b, j: (0, 0), pipeline_mode=pl.Buffered(1)),
                  tab, tab, tab],
        out_specs=a_specs + kv_specs,
        out_shape=a_shapes + kv_shapes,
        scratch_shapes=[pltpu.VMEM((3 * ATT_WIDTH // LANES, tm, LANES), F32)] * N_GROUPS,
        compiler_params=_cparams("parallel", "arbitrary"),
        name="attn_qkv_rope_long",
    )(x, g.reshape(1, d), wqkv, *tables)
    return outs[:N_GROUPS], outs[N_GROUPS:]


def _attn_prompt_kernel(q_ref, kp_ref, kc_ref, vp_ref, vc_ref, o_ref, l_ref, *, nr, nq):
    blk = ATT_BLK
    qi = lax.broadcasted_iota(jnp.int32, (blk, 2 * blk), 0)
    kj = lax.broadcasted_iota(jnp.int32, (blk, 2 * blk), 1)
    dist = qi + blk - kj
    band = (dist >= 0) & (dist <= blk)
    first = band & (pl.program_id(2) * nq * blk + kj - blk >= 0)
    low = lax.broadcasted_iota(jnp.int32, (blk, LANES), 1) < ATT_HEAD_DIM
    nt = (((1,), (1,)), ((), ()))
    for r in range(nr):
        for sub in range(nq):
            rows = slice(sub * blk, (sub + 1) * blk)
            prows = slice((sub - 1) * blk, sub * blk)
            valid = first if sub == 0 else band
            q = q_ref[0, r, rows, :]
            for hp in range(ATT_WIDTH // LANES):
                cs = slice(hp * LANES, (hp + 1) * LANES)
                kprev = kp_ref[0, r, :, cs] if sub == 0 else kc_ref[0, r, prows, cs]
                vprev = vp_ref[0, r, :, cs] if sub == 0 else vc_ref[0, r, prows, cs]
                kp = jnp.concatenate([kprev, kc_ref[0, r, rows, cs]], axis=0)
                vp = jnp.concatenate([vprev, vc_ref[0, r, rows, cs]], axis=0)
                qp = q[:, cs]
                outs = []
                for par in range(2):
                    qm = jnp.where(low if par == 0 else jnp.logical_not(low), qp,
                                   jnp.zeros_like(qp))
                    s = lax.dot_general(qm, kp, nt, preferred_element_type=F32)
                    s = jnp.where(valid, s, -jnp.inf)
                    m = jnp.max(s, axis=-1, keepdims=True)
                    p = jnp.exp(s - m)
                    l = jnp.sum(p, axis=-1, keepdims=True)
                    outs.append((_dot(p.astype(BF16), vp) / l,
                                 jnp.broadcast_to(m + jnp.log(l), (blk, LANES))))
                o_ref[0, r, rows, cs] = jnp.where(low, outs[0][0], outs[1][0]).astype(BF16)
                l_ref[0, r, rows, cs] = jnp.where(low, outs[0][1], outs[1][1])


def _attn_prompt_call(a, gi):
    b, dil, n_sub, _ = a.shape
    nb = n_sub // ATT_BLK
    nq = min(nb, ATT_STEP_BLOCKS)
    nr = min(dil, ATT_STEP_BLOCKS // nq)
    curshape = (1, nr, nq * ATT_BLK, ATT_WIDTH)

    def cur(part):
        return pl.BlockSpec(curshape, lambda i, r, n: (i, r, n, part))

    def prv(part):
        return pl.BlockSpec((1, nr, ATT_BLK, ATT_WIDTH),
                            lambda i, r, n: (i, r, jnp.maximum(n * nq - 1, 0), part))

    return pl.pallas_call(
        functools.partial(_attn_prompt_kernel, nr=nr, nq=nq),
        grid=(b, dil // nr, nb // nq),
        in_specs=[cur(0), prv(1), cur(1), prv(2), cur(2)],
        out_specs=[cur(0), cur(0)],
        out_shape=[jax.ShapeDtypeStruct((b, dil, n_sub, ATT_WIDTH), BF16),
                   jax.ShapeDtypeStruct((b, dil, n_sub, ATT_WIDTH), F32)],
        compiler_params=_cparams("parallel", "parallel", "parallel"),
        name=f"attn_prompt_g{gi}",
    )(a, a, a, a, a)


def _attn_sample_kernel(qkv_ref, c0_ref, c1_ref, c2_ref, o_ref, *, t):
    caches = (c0_ref, c1_ref, c2_ref)
    nrow = ATT_HEADS * t
    rowi = lax.broadcasted_iota(jnp.int32, (nrow, ATT_WIDTH), 0)
    coli = lax.broadcasted_iota(jnp.int32, (nrow, ATT_WIDTH), 1)
    headmask = (rowi // t) == (coli // ATT_HEAD_DIM)
    pad = jnp.zeros((LANES - t, ATT_WIDTH), F32)
    nt = (((1,), (1,)), ((), ()))
    stats = []
    for gi in range(N_GROUPS):
        window, dil = ATT_GROUPS[gi]
        cref = caches[gi]
        ln = cref.shape[-1]
        q = qkv_ref[0, :, gi * ATT_WIDTH:(gi + 1) * ATT_WIDTH] * ATT_SCALE
        knew = qkv_ref[0, :, (N_GROUPS + gi) * ATT_WIDTH:(N_GROUPS + gi + 1) * ATT_WIDTH]
        vnew = qkv_ref[0, :, (2 * N_GROUPS + gi) * ATT_WIDTH:(2 * N_GROUPS + gi + 1) * ATT_WIDTH]
        qexp = jnp.where(headmask, jnp.concatenate([q] * ATT_HEADS, axis=0), 0.0).astype(BF16)
        kn = jnp.concatenate([knew, pad], axis=0).astype(BF16)
        vn = jnp.concatenate([vnew, pad], axis=0).astype(BF16)
        sc = _dot(qexp, cref[0, 0].astype(BF16))
        sn = lax.dot_general(qexp, kn, nt, preferred_element_type=F32)
        qi_c = lax.broadcasted_iota(jnp.int32, (nrow, ln), 0) % t
        pos_c = lax.broadcasted_iota(jnp.int32, (nrow, ln), 1)
        delta_c = ln + qi_c - pos_c
        ok_c = (delta_c <= window) & (delta_c % dil == 0)
        qi_n = lax.broadcasted_iota(jnp.int32, (nrow, LANES), 0) % t
        j_n = lax.broadcasted_iota(jnp.int32, (nrow, LANES), 1)
        delta_n = qi_n - j_n
        ok_n = (j_n < t) & (delta_n >= 0) & (delta_n <= window) & (delta_n % dil == 0)
        sc = jnp.where(ok_c, sc, -jnp.inf)
        sn = jnp.where(ok_n, sn, -jnp.inf)
        m = jnp.maximum(jnp.max(sc, axis=-1, keepdims=True), jnp.max(sn, axis=-1, keepdims=True))
        pc = jnp.exp(sc - m)
        pn = jnp.exp(sn - m)
        l = jnp.sum(pc, axis=-1, keepdims=True) + jnp.sum(pn, axis=-1, keepdims=True)
        og = (lax.dot_general(pc.astype(BF16), cref[0, 1].astype(BF16), nt,
                              preferred_element_type=F32)
              + _dot(pn.astype(BF16), vn))
        stats.append((m + jnp.log(l), l, og))

    mx = functools.reduce(jnp.maximum, [s[0] for s in stats])
    es = [jnp.exp(s[0] - mx) for s in stats]
    den = es[0] + es[1] + es[2]
    full = sum(s[2] * (e / (den * s[1])) for s, e in zip(stats, es))
    for h in range(ATT_HEADS):
        hs = slice(h * ATT_HEAD_DIM, (h + 1) * ATT_HEAD_DIM)
        o_ref[0, :, hs] = full[h * t:(h + 1) * t, hs]


def _attn_sample_call(qkv, caches, b, t):
    nq = qkv.shape[-1]
    views = [c.transpose(0, 2, 3, 4, 1).reshape(b, 2, ATT_WIDTH, c.shape[1]) for c in caches]
    return pl.pallas_call(
        functools.partial(_attn_sample_kernel, t=t),
        grid=(b,),
        in_specs=[pl.BlockSpec((1, t, nq), lambda i: (i, 0, 0))]
                 + [pl.BlockSpec((1,) + v.shape[1:], lambda i: (i, 0, 0, 0)) for v in views],
        out_specs=pl.BlockSpec((1, t, ATT_WIDTH), lambda i: (i, 0, 0)),
        out_shape=jax.ShapeDtypeStruct((b, t, ATT_WIDTH), F32),
        compiler_params=_cparams("parallel"),
        name="attn_sample",
    )(qkv.reshape(b, t, nq), *views)


def _attn_out_kernel(*refs, merge, tm):
    if merge:
        o_refs = refs[:N_GROUPS]
        l_refs = refs[N_GROUPS:2 * N_GROUPS]
        x_ref, g_ref, wo_ref, out_ref = refs[2 * N_GROUPS:2 * N_GROUPS + 4]
        scr = refs[2 * N_GROUPS + 4:]
        vals = []
        for gi, ref in enumerate(o_refs + l_refs):
            dil = ATT_GROUPS[gi % N_GROUPS][1]
            if dil == 1:
                vals.append(ref[0, 0])
            else:
                for cb in range(ATT_WIDTH // LANES):
                    for res in range(dil):
                        scr[gi][cb, pl.ds(res, tm // dil, stride=dil), :] = (
                            ref[0, res, :, cb * LANES:(cb + 1) * LANES].astype(F32))
                vals.append(jnp.concatenate([scr[gi][cb] for cb in range(ATT_WIDTH // LANES)], axis=1))
        oa, ob, oc, la, lb, lc = vals
        mx = jnp.maximum(jnp.maximum(la, lb), lc)
        ea, eb, ec = jnp.exp(la - mx), jnp.exp(lb - mx), jnp.exp(lc - mx)
        den = ea + eb + ec
        o = (ea / den) * oa + (eb / den) * ob + (ec / den) * oc
    else:
        o0, x_ref, g_ref, wo_ref, out_ref = refs
        o = o0[...]
    y = _dot(o.astype(BF16), wo_ref[...])
    out_ref[...] = x_ref[...] + _rms(y, g_ref[...])


def _attn_out_call(os_, lses, x, g, wo, tm, batch=None):
    n, d = x.shape
    merge = lses is not None
    kern = functools.partial(_attn_out_kernel, merge=merge, tm=tm)
    if not merge:
        row = pl.BlockSpec((tm, d), lambda i: (i, 0))
        return pl.pallas_call(
            kern,
            grid=(n // tm,),
            in_specs=[pl.BlockSpec((tm, ATT_WIDTH), lambda i: (i, 0)), row,
                      pl.BlockSpec((1, d), lambda i: (0, 0)), pl.BlockSpec(wo.shape, lambda i: (0, 0))],
            out_specs=row,
            out_shape=jax.ShapeDtypeStruct((n, d), F32),
            compiler_params=_cparams("parallel"),
            name="attn_out_short",
        )(os_[0], x, g.reshape(1, d), wo)
    tpb = n // batch // tm
    row = pl.BlockSpec((tm, d), lambda b, j: (b * tpb + j, 0))
    aspecs = [pl.BlockSpec((1, dil, tm // dil, ATT_WIDTH), lambda b, j: (b, 0, j, 0))
              for _, dil in ATT_GROUPS]
    return pl.pallas_call(
        kern,
        grid=(batch, tpb),
        in_specs=aspecs * 2 + [row, pl.BlockSpec((1, d), lambda b, j: (0, 0)),
                               pl.BlockSpec(wo.shape, lambda b, j: (0, 0))],
        out_specs=row,
        out_shape=jax.ShapeDtypeStruct((n, d), F32),
        scratch_shapes=[pltpu.VMEM((ATT_WIDTH // LANES, tm, LANES), F32)] * (2 * N_GROUPS),
        compiler_params=_cparams("parallel", "parallel"),
        name="attn_out_long",
    )(*os_, *lses, x, g.reshape(1, d), wo)


def _pad_rows8(a):
    return jnp.concatenate([a, jnp.zeros((SUBLANES - a.shape[0],) + a.shape[1:], a.dtype)], axis=0)


def _pad_to(a, axis, size):
    padw = [(0, 0)] * a.ndim
    padw[axis] = (0, size - a.shape[axis])
    return jnp.pad(a, padw)


def _prep_weights(norm_g, rw, at_wqkv, at_wo, ff):
    (mu, wr, wk, wv, wo, w0, w1, w2, a0, a1, a2, g1, g2, k_k, k_a, r_k, lnx_g, lnx_b) = [p[0] for p in rw]
    glora = 2 * LANES
    sc = lambda a: _to_scan_heads(a.astype(BF16))
    proj_ws = [sc(wr), sc(wk), sc(wv),
               _pad_to(w1, 1, LANES).astype(BF16), _pad_to(sc(w2), 0, LANES),
               _pad_to(a1, 1, LANES).astype(BF16), _pad_to(sc(a2), 0, LANES),
               _pad_to(g1, 1, glora).astype(BF16), _pad_to(g2, 0, glora).astype(BF16)]
    return dict(
        mu8=_pad_rows8(mu),
        proj_vec=_pad_rows8(_to_scan_heads(jnp.stack([w0, a0, k_k, k_a, r_k.reshape(-1)]))),
        proj_ws=proj_ws,
        out_vec=_pad_rows8(jnp.stack([lnx_g, lnx_b, norm_g[0, 1]])),
        rw_wo=wo.astype(BF16),
        wqkv=at_wqkv[0].astype(BF16),
        at_wo=at_wo[0].astype(BF16),
        ff=[(_pad_rows8(jnp.stack([norm_g[l, 2], norm_g[l, 3]])), ff[0][l].astype(BF16),
             _pad_rows8(ff[1][l]), _pad_rows8(ff[2][l][None]), ff[3][l].astype(BF16))
            for l in range(2)],
        ones=_seg_ones(),
    )


def _run_group(x, pos, shift0, wkv0, kv_bufs, conv0, norm_g, w, tm):
    b, t, d = x.shape
    n = b * t
    tm = min(tm, n)
    long_mode = shift0 is None
    xf = x.reshape(n, d)
    gn0 = norm_g[0, 0].reshape(1, d)
    to3 = lambda a: a.reshape(b, t, d)

    if long_mode:
        r, dec, k, kk, ka, g, bonus, v, hl = _rwkv_proj_call(xf, gn0, w, min(tm, 256), batch=b)
        new_shift = hl[:, SUBLANES - 1][None]
        s0 = jnp.zeros((b, SCAN_ROWS, LANES), F32)
    else:
        sh = jnp.concatenate([shift0[0][:, None, :], jnp.zeros((b, t - 1, d), F32)], axis=1)
        r, dec, k, kk, ka, g, bonus, v, hn = _rwkv_proj_call(
            xf, gn0, w, tm, seq_len=t, shift_rows=sh.reshape(n, d))
        new_shift = to3(hn)[:, -1][None]
        s0 = _state_to_scan(wkv0[0])
    y, s_fin = _scan_call(to3(kk), to3(dec), to3(k), to3(ka), to3(r), to3(v), s0,
                          min(t, SCAN_CHUNK))
    x1 = _rwkv_out_call(y.reshape(n, d), bonus, g, xf, w["out_vec"], w["rw_wo"], w["ones"], tm)
    new_wkv = _state_from_scan(s_fin)[None]

    def ffn(xin, layer):
        g2, wup, cw8, cb8, wdown = w["ff"][layer]
        if long_mode:
            ftm = min(t, FF_LONG_TILE)
            xo, hl = _ffn_call(xin, g2, wup, cw8, cb8, wdown, ftm, batch=b)
            cs = hl.reshape(b, t // ftm, SUBLANES, 2 * D_FF)[:, -1, SUBLANES - (CONV_W - 1):]
        else:
            st = conv0[layer]
            z = jnp.zeros((b, t - 1, 2 * D_FF), F32)
            p1 = jnp.concatenate([st[:, 1:2], z], axis=1).reshape(n, 2 * D_FF)
            p2 = jnp.concatenate([st, z[:, 1:]], axis=1).reshape(n, 2 * D_FF)
            xo, hl = _ffn_call(xin, g2, wup, cw8, cb8, wdown, tm, seq_len=t, prev=(p1, p2))
            cs = hl.reshape(b, t, 2 * D_FF)[:, t - (CONV_W - 1):]
        return xo, cs

    x2, conv_a = ffn(x1, 0)

    new_kv = []
    if long_mode:
        a_g, kv_g = _qkv_call(x2, norm_g[1, 0], w["wqkv"], _rope_tables(pos), tm, batch=b)
        for kv in kv_g:
            new_kv.append(kv.reshape(1, b, kv.shape[1], 2, ATT_HEADS, ATT_HEAD_DIM))
        res = [_attn_prompt_call(a_g[gi], gi) for gi in range(N_GROUPS)]
        x3 = _attn_out_call([o for o, _ in res], [l for _, l in res], x2, norm_g[1, 1],
                            w["at_wo"], tm, batch=b)
    else:
        tables = tuple(np.tile(a, (tm // t, 1)) for a in _rope_tables(pos))
        qkv = _qkv_call(x2, norm_g[1, 0], w["wqkv"], tables, tm)
        qkv3 = qkv.reshape(b, t, 3 * N_GROUPS, ATT_HEADS, ATT_HEAD_DIM)
        for gi in range(N_GROUPS):
            new_kv.append(jnp.stack([qkv3[:, :, N_GROUPS + gi], qkv3[:, :, 2 * N_GROUPS + gi]],
                                    axis=2)[None])
        o = _attn_sample_call(qkv, [buf[0] for buf in kv_bufs], b, t)
        x3 = _attn_out_call([o.reshape(n, ATT_WIDTH)], None, x2, norm_g[1, 1], w["at_wo"], tm)
    x4, conv_b = ffn(x3, 1)
    return (x4.reshape(b, t, d), new_shift, new_wkv, new_kv, jnp.stack([conv_a, conv_b]))


def kernel(x_prompt, x_sample, state_shift, state_wkv, cache_kv_w128, cache_kv_w512, cache_kv_w2048, state_conv, norm_g, rw_mu, rw_wr, rw_wk, rw_wv, rw_wo, rw_w0, rw_w1, rw_w2, rw_a0, rw_a1, rw_a2, rw_g1, rw_g2, rw_kk, rw_ka, rw_rk, rw_lnx_g, rw_lnx_b, at_wqkv, at_wo, ff_wup, ff_conv_w, ff_conv_b, ff_wdown):
    rw = (rw_mu, rw_wr, rw_wk, rw_wv, rw_wo, rw_w0, rw_w1, rw_w2, rw_a0, rw_a1, rw_a2,
          rw_g1, rw_g2, rw_kk, rw_ka, rw_rk, rw_lnx_g, rw_lnx_b)
    ff = (ff_wup, ff_conv_w, ff_conv_b, ff_wdown)
    w = _prep_weights(norm_g, rw, at_wqkv, at_wo, ff)
    tp = x_prompt.shape[1]
    ts = x_sample.shape[1]
    pos_p = np.arange(tp)
    y_p, p_shift, p_wkv, p_kv, p_conv = _run_group(
        x_prompt, pos_p, None, None, None, None, norm_g, w, 512)
    pos_s = PAST_LEN + np.arange(ts)
    y_s, s_shift, s_wkv, s_kv, s_conv = _run_group(
        x_sample, pos_s, state_shift, state_wkv,
        (cache_kv_w128, cache_kv_w512, cache_kv_w2048), state_conv, norm_g, w, 256)
    return (y_p, y_s, p_shift, p_wkv, p_kv[0], p_kv[1], p_kv[2], p_conv,
            s_shift, s_wkv, s_kv[0], s_kv[1], s_kv[2], s_conv)
```

```python
import functools
import math

import jax
import jax.numpy as jnp
import numpy as np
from jax import lax
from jax.experimental import pallas as pl
from jax.experimental.pallas import tpu as pltpu

F32 = jnp.float32
BF16 = jnp.bfloat16

D_MODEL = 1024
RW_HEAD = 64
RW_HEADS = D_MODEL // RW_HEAD
RW_GN_EPS = 64e-5
ATT_GROUPS = ((128, 1), (512, 4), (2048, 16))
N_GROUPS = 3
ATT_HEADS = 8
ATT_HEAD_DIM = 64
ATT_WIDTH = ATT_HEADS * ATT_HEAD_DIM
ATT_SCALE = ATT_HEAD_DIM ** -0.5
ATT_BLK = 128
ATT_STEP_BLOCKS = 4
ROT_DIM = ATT_HEAD_DIM // 4
ROPE_THETA = 500000.0
D_FF = 2816
CONV_W = 3
NORM_EPS = 1e-6
PAST_LEN = 2048

LANES = 128
SUBLANES = 8
FF_CHUNK = 256
FF_DOWN_CHUNKS = 11
FF_LONG_TILE = 512
SCAN_CHUNK = 128
SCAN_HALF = 64
SCAN_ROWS = RW_HEADS // 2 * RW_HEAD
SCAN_NB = 2
SCAN_NB_SHORT = 4
VMEM_LIMIT = 56 * 1024 * 1024


def _cparams(*sem):
    return pltpu.CompilerParams(dimension_semantics=sem, vmem_limit_bytes=VMEM_LIMIT)


def _dot(a, b):
    return jnp.dot(a, b, preferred_element_type=F32)


def _rms(x, g):
    ms = jnp.mean(x * x, axis=-1, keepdims=True)
    return x * lax.rsqrt(ms + NORM_EPS) * g


def _segsum(x, ones):
    outs = []
    for c in range(x.shape[1] // LANES):
        xc = x[:, c * LANES:(c + 1) * LANES]
        hi = xc.astype(BF16)
        lo = (xc - hi.astype(F32)).astype(BF16)
        outs.append(_dot(hi, ones) + _dot(lo, ones))
    return jnp.concatenate(outs, axis=1)


def _scan_to_natural_heads(x):
    low = lax.broadcasted_iota(jnp.int32, (x.shape[0], LANES), 1) < RW_HEAD
    outs = []
    for c in range(0, x.shape[1] // LANES, 2):
        a = x[:, c * LANES:(c + 1) * LANES]
        b = x[:, (c + 1) * LANES:(c + 2) * LANES]
        outs.append(jnp.where(low, a, pltpu.roll(b, RW_HEAD, axis=1)))
        outs.append(jnp.where(low, pltpu.roll(a, RW_HEAD, axis=1), b))
    return jnp.concatenate(outs, axis=1)


def _rwkv_proj_kernel(*refs, seq_len, tm):
    if seq_len is None:
        (x_ref, gn_ref, mu_ref, vec_ref, wr_ref, wk_ref, wv_ref, w1_ref, w2_ref, a1_ref,
         a2_ref, g1_ref, g2_ref, ones_ref,
         r_ref, d_ref, k_ref, kk_ref, ka_ref, g_ref, bonus_ref, v_ref, hl_ref, carry) = refs
    else:
        (x_ref, gn_ref, mu_ref, vec_ref, wr_ref, wk_ref, wv_ref, w1_ref, w2_ref, a1_ref,
         a2_ref, g1_ref, g2_ref, ones_ref, sh_ref,
         r_ref, d_ref, k_ref, kk_ref, ka_ref, g_ref, bonus_ref, v_ref, hn_ref) = refs
    hn = _rms(x_ref[...], gn_ref[...])
    rolled = pltpu.roll(hn, 1, axis=0)
    rows = lax.broadcasted_iota(jnp.int32, (tm, D_MODEL), 0)
    if seq_len is None:
        @pl.when(pl.program_id(1) == 0)
        def _():
            carry[...] = jnp.zeros(carry.shape, F32)

        prev = jnp.where(rows == 0, carry[0:1, :], rolled)
        carry[0:1, :] = hn[tm - 1:tm, :]
        hl_ref[0] = hn[tm - SUBLANES:tm, :]
    else:
        prev = jnp.where(rows % seq_len == 0, sh_ref[...], rolled)
        hn_ref[...] = hn
    xx = prev - hn

    def mix(i):
        return (hn + xx * mu_ref[i:i + 1, :]).astype(BF16)

    w0 = vec_ref[0:1, :]
    a0 = vec_ref[1:2, :]
    k_k = vec_ref[2:3, :]
    k_a = vec_ref[3:4, :]
    r_k = vec_ref[4:5, :]

    r = _dot(mix(0), wr_ref[...])
    k = _dot(mix(2), wk_ref[...])
    v = _dot(mix(3), wv_ref[...])
    wl = _dot(jnp.tanh(_dot(mix(1), w1_ref[...])).astype(BF16), w2_ref[...])
    z = -(w0 + wl)
    softplus = jnp.maximum(z, 0.0) + jnp.log1p(jnp.exp(-jnp.abs(z)))
    w = -softplus - 0.5
    decay = jnp.exp(-jnp.exp(w))
    al = _dot(_dot(mix(4), a1_ref[...]).astype(BF16), a2_ref[...])
    a = jax.nn.sigmoid(a0 + al)
    g = _dot(jax.nn.sigmoid(_dot(mix(5), g1_ref[...])).astype(BF16), g2_ref[...])

    ones = ones_ref[...]
    kk = k * k_k
    nrm = jnp.sqrt(_segsum(kk * kk, ones))
    kk = kk / jnp.maximum(nrm, 1e-12)
    k2 = k * (1.0 + (a - 1.0) * k_a)

    r_ref[...] = r
    d_ref[...] = decay
    k_ref[...] = k2
    kk_ref[...] = kk
    ka_ref[...] = kk * a
    g_ref[...] = g.astype(BF16)
    bonus_ref[...] = _scan_to_natural_heads(_segsum(r * k2 * r_k, ones) * v).astype(BF16)
    v_ref[...] = v


def _rwkv_proj_call(x, gn, w, tm, batch=None, seq_len=None, shift_rows=None):
    n, d = x.shape
    nd = [jax.ShapeDtypeStruct((n, d), F32)]
    ndh = [jax.ShapeDtypeStruct((n, d), BF16)]
    ws = w["proj_ws"]
    if seq_len is None:
        t = n // batch
        tpb = t // tm
        row = pl.BlockSpec((tm, d), lambda b, j: (b * tpb + j, 0))

        def full(a):
            return pl.BlockSpec(a.shape, lambda b, j: (0,) * a.ndim, pipeline_mode=pl.Buffered(1))

        ins = [x, gn, w["mu8"], w["proj_vec"]] + ws + [w["ones"]]
        return pl.pallas_call(
            functools.partial(_rwkv_proj_kernel, seq_len=None, tm=tm),
            grid=(batch, tpb),
            in_specs=[row] + [full(a) for a in ins[1:]],
            out_specs=[row] * 8 + [pl.BlockSpec((1, SUBLANES, d), lambda b, j: (b, 0, 0))],
            out_shape=nd * 5 + ndh * 2 + nd + [jax.ShapeDtypeStruct((batch, SUBLANES, d), F32)],
            scratch_shapes=[pltpu.VMEM((SUBLANES, d), F32)],
            compiler_params=_cparams("parallel", "arbitrary"),
            name="rwkv_proj_long",
        )(*ins)
    row = pl.BlockSpec((tm, d), lambda i: (i, 0))

    def full(a):
        return pl.BlockSpec(a.shape, lambda i: (0,) * a.ndim)

    ins = [x, gn, w["mu8"], w["proj_vec"]] + ws + [w["ones"]]
    return pl.pallas_call(
        functools.partial(_rwkv_proj_kernel, seq_len=seq_len, tm=tm),
        grid=(n // tm,),
        in_specs=[row] + [full(a) for a in ins[1:]] + [row],
        out_specs=[row] * 9,
        out_shape=nd * 5 + ndh * 2 + nd * 2,
        compiler_params=_cparams("parallel"),
        name="rwkv_proj_short",
    )(*ins, shift_rows)


def _scan_kernel(kk_ref, d_ref, k_ref, ka_ref, r_ref, v_ref, s0_ref, selv_ref, place_ref,
                 ones_ref, y_ref, sout_ref, s_scr, lv_scr, yt_scr, *, steps, nb):
    c = pl.program_id(1)

    @pl.when(c == 0)
    def _():
        s_scr[...] = s0_ref[...]

    ones = ones_ref[...]
    low = lax.broadcasted_iota(jnp.int32, (SCAN_ROWS, LANES), 1) < RW_HEAD
    n_half = -(-steps // SCAN_HALF)
    for b in range(nb):
        vb = v_ref[b]
        if steps < LANES:
            vb = jnp.concatenate([vb, jnp.zeros((LANES - steps, D_MODEL), F32)], axis=0)
        tr = [vb[:, hp * LANES:(hp + 1) * LANES].T for hp in range(RW_HEADS // 2)]
        ev = jnp.concatenate([x[0:RW_HEAD] for x in tr], axis=0)
        od = jnp.concatenate([x[RW_HEAD:2 * RW_HEAD] for x in tr], axis=0)
        lv_scr[b, 0] = jnp.where(low, ev, pltpu.roll(od, RW_HEAD, axis=1)).astype(BF16)
        if n_half > 1:
            lv_scr[b, 1] = jnp.where(low, pltpu.roll(ev, RW_HEAD, axis=1), od).astype(BF16)

    def rows(blk, j):
        return jnp.concatenate(
            [jnp.broadcast_to(blk[j:j + 1, hp * LANES:(hp + 1) * LANES], (RW_HEAD, LANES))
             for hp in range(RW_HEADS // 2)], axis=0)

    for half in range(n_half):
        nst = min(SCAN_HALF, steps - half * SCAN_HALF)
        yt_scr[...] = jnp.zeros(yt_scr.shape, F32)

        def body(grp, carry, half=half):
            base = pl.multiple_of(half * SCAN_HALF + grp * SUBLANES, SUBLANES)
            blks = [[ref[b, pl.ds(base, SUBLANES), :]
                     for ref in (kk_ref, d_ref, k_ref, ka_ref, r_ref)] for b in range(nb)]
            p2 = [[] for _ in range(nb)]
            for j in range(SUBLANES):
                for b in range(nb):
                    kk8, d8, k8, ka8, r8 = blks[b]
                    s = s_scr[b]
                    sk = _dot((s * rows(kk8, j)).astype(BF16), ones)
                    vcol = _dot(lv_scr[b, half], selv_ref[grp * SUBLANES + j])
                    s = s * rows(d8, j) - sk * rows(ka8, j) + vcol * rows(k8, j)
                    s_scr[b] = s
                    p2[b].append((s * rows(r8, j)).astype(BF16))
            prow = pl.multiple_of(grp * SUBLANES * LANES, SUBLANES * LANES)
            pl8 = place_ref[pl.ds(prow, SUBLANES * LANES), :]
            for b in range(nb):
                yt_scr[b] += _dot(jnp.concatenate(p2[b], axis=1), pl8)
            return carry

        lax.fori_loop(0, nst // SUBLANES, body, 0)

        t0 = half * SCAN_HALF
        for b in range(nb):
            yt = yt_scr[b]
            for q in range(RW_HEADS // 4):
                tr = yt[q * 2 * RW_HEAD:(q + 1) * 2 * RW_HEAD, :].T
                c0 = q * 2 * LANES
                y_ref[b, t0:t0 + nst, c0:c0 + LANES] = tr[0:nst]
                y_ref[b, t0:t0 + nst, c0 + LANES:c0 + 2 * LANES] = tr[SCAN_HALF:SCAN_HALF + nst]

    @pl.when(c == pl.num_programs(1) - 1)
    def _():
        sout_ref[...] = s_scr[...]


def _scan_tables():
    col = np.arange(LANES)
    row = np.arange(LANES)
    t = np.arange(SCAN_HALF)
    selv = ((row[None, :, None] // RW_HEAD == col[None, None, :] // RW_HEAD)
            & (row[None, :, None] % RW_HEAD == t[:, None, None]))
    place = ((row[None, :, None] // RW_HEAD == col[None, None, :] // RW_HEAD)
             & (col[None, None, :] % RW_HEAD == t[:, None, None]))
    return (jnp.asarray(selv, BF16),
            jnp.asarray(place.reshape(SCAN_HALF * LANES, LANES), BF16))


def _seg_ones():
    i = np.arange(LANES)
    return jnp.asarray(i[:, None] // RW_HEAD == i[None, :] // RW_HEAD, BF16)


def _scan_call(kk, dec, k, ka, r, v, s0, steps):
    b, t, d = kk.shape
    assert steps % SUBLANES == 0 and t % steps == 0
    nb = SCAN_NB if steps == SCAN_CHUNK else SCAN_NB_SHORT
    nb = nb if b % nb == 0 else 1
    nc = t // steps
    n_half = -(-steps // SCAN_HALF)
    selv, place = _scan_tables()
    ones = _seg_ones()
    rowspec = pl.BlockSpec((nb, steps, d), lambda i, c: (i, c, 0))

    def full(a):
        return pl.BlockSpec(a.shape, lambda i, c: (0,) * a.ndim)

    sspec = pl.BlockSpec((nb, SCAN_ROWS, LANES), lambda i, c: (i, 0, 0))
    return pl.pallas_call(
        functools.partial(_scan_kernel, steps=steps, nb=nb),
        grid=(b // nb, nc),
        in_specs=[rowspec] * 6 + [sspec, full(selv), full(place), full(ones)],
        out_specs=[rowspec, sspec],
        out_shape=[jax.ShapeDtypeStruct((b, t, d), F32),
                   jax.ShapeDtypeStruct((b, SCAN_ROWS, LANES), F32)],
        scratch_shapes=[pltpu.VMEM((nb, SCAN_ROWS, LANES), F32),
                        pltpu.VMEM((nb, n_half, SCAN_ROWS, LANES), BF16),
                        pltpu.VMEM((nb, SCAN_ROWS, LANES), F32)],
        compiler_params=_cparams("parallel", "arbitrary"),
        name="rwkv_scan",
    )(kk, dec, k, ka, r, v, s0, selv, place, ones)


def _to_scan_heads(a):
    lead = a.shape[:-1]
    return a.reshape(lead + (RW_HEADS // 4, 2, 2, RW_HEAD)).swapaxes(-3, -2).reshape(
        lead + (D_MODEL,))


def _state_to_scan(s):
    b = s.shape[0]
    s = s.reshape(b, RW_HEADS // 4, 2, 2, RW_HEAD, RW_HEAD)
    return s.transpose(0, 1, 3, 4, 2, 5).reshape(b, SCAN_ROWS, LANES)


def _state_from_scan(s):
    b = s.shape[0]
    s = s.reshape(b, RW_HEADS // 4, 2, RW_HEAD, 2, RW_HEAD)
    return s.transpose(0, 1, 4, 2, 3, 5).reshape(b, RW_HEADS, RW_HEAD, RW_HEAD)


def _rwkv_out_kernel(y_ref, bonus_ref, g_ref, x_ref, vec_ref, wo_ref, ones_ref, o_ref):
    ones = ones_ref[...]
    y = y_ref[...]
    mean = _segsum(y, ones) * (1.0 / RW_HEAD)
    yc = y - mean
    var = _segsum(yc * yc, ones) * (1.0 / RW_HEAD)
    yn = yc * lax.rsqrt(var + RW_GN_EPS) * vec_ref[0:1, :] + vec_ref[1:2, :]
    z = ((yn + bonus_ref[...]) * g_ref[...]).astype(BF16)
    mo = _dot(z, wo_ref[...])
    o_ref[...] = x_ref[...] + _rms(mo, vec_ref[2:3, :])


def _rwkv_out_call(y, bonus, g, x, vec8, wo, ones, tm):
    n, d = y.shape
    row = pl.BlockSpec((tm, d), lambda i: (i, 0))

    def full(a):
        return pl.BlockSpec(a.shape, lambda i: (0,) * a.ndim)

    return pl.pallas_call(
        _rwkv_out_kernel,
        grid=(n // tm,),
        in_specs=[row] * 4 + [full(vec8), full(wo), full(ones)],
        out_specs=row,
        out_shape=jax.ShapeDtypeStruct((n, d), F32),
        compiler_params=_cparams("parallel"),
        name="rwkv_out",
    )(y, bonus, g, x, vec8, wo, ones)


def _gelu_tanh(x):
    return 0.5 * x * (1.0 + jnp.tanh(math.sqrt(2.0 / math.pi) * (x + 0.044715 * (x * x * x))))


def _ffn_kernel(*refs, seq_len, tm):
    if seq_len is None:
        (x_ref, g_ref, wup_ref, cw_ref, cb_ref, wdown_ref, xo_ref, hl_ref,
         carry, hs_scr, act_scr) = refs
    else:
        x_ref, g_ref, wup_ref, cw_ref, cb_ref, wdown_ref, prev_ref, xo_ref, hl_ref = refs
    x = x_ref[...]
    hn = _rms(x, g_ref[0:1, :]).astype(BF16)
    n_chunks = D_FF // FF_CHUNK
    if seq_len is None:
        @pl.when(pl.program_id(1) == 0)
        def _():
            carry[...] = jnp.zeros(carry.shape, F32)
    else:
        rows = lax.broadcasted_iota(jnp.int32, (tm, FF_CHUNK), 0)
        tpos = rows % seq_len

    acc = jnp.zeros((tm, D_MODEL), F32)
    seg0 = 0
    for j in range(n_chunks):
        conv = []
        for part in range(2):
            c0 = part * D_FF + j * FF_CHUNK
            cs = slice(c0, c0 + FF_CHUNK)
            h = _dot(hn, wup_ref[:, cs])
            if seq_len is None:
                buf = (j % 2) * 2 + part
                hs_scr[buf, 0:SUBLANES, :] = carry[:, cs]
                hs_scr[buf, SUBLANES:SUBLANES + tm, :] = h
                carry[:, cs] = h[tm - SUBLANES:tm, :]
                hl_ref[:, cs] = h[tm - SUBLANES:tm, :]
                h1 = hs_scr[buf, SUBLANES - 1:SUBLANES - 1 + tm, :]
                h2 = hs_scr[buf, SUBLANES - 2:SUBLANES - 2 + tm, :]
            else:
                pv = prev_ref[:, cs]
                h1 = jnp.where(tpos == 0, pltpu.roll(pv, tm - 1, axis=0), pltpu.roll(h, 1, axis=0))
                h2 = jnp.where(tpos < 2, pv, pltpu.roll(h, 2, axis=0))
                hl_ref[:, cs] = h
            conv.append(cb_ref[0:1, cs] + h2 * cw_ref[0:1, cs] + h1 * cw_ref[1:2, cs]
                        + h * cw_ref[2:3, cs])
        act = (_gelu_tanh(conv[0]) * conv[1]).astype(BF16)
        if seq_len is None:
            act_scr[:, j * FF_CHUNK:(j + 1) * FF_CHUNK] = act
            if (j + 1) % FF_DOWN_CHUNKS == 0 or j == n_chunks - 1:
                seg = slice(seg0 * FF_CHUNK, (j + 1) * FF_CHUNK)
                acc = acc + _dot(act_scr[:, seg], wdown_ref[seg, :])
                seg0 = j + 1
        else:
            acc = acc + _dot(act, wdown_ref[j * FF_CHUNK:(j + 1) * FF_CHUNK, :])
    xo_ref[...] = x + _rms(acc, g_ref[1:2, :])


def _ffn_call(x, g2, wup, cw8, cb8, wdown, tm, layer, seq_len=None, prev=None, batch=None):
    n, d = x.shape
    f2 = 2 * D_FF
    kern = functools.partial(_ffn_kernel, seq_len=seq_len, tm=tm)
    if seq_len is None:
        tpb = n // batch // tm
        row = pl.BlockSpec((tm, d), lambda b, j: (b * tpb + j, 0))

        def full(a):
            return pl.BlockSpec(a.shape, lambda b, j: (0,) * a.ndim, pipeline_mode=pl.Buffered(1))

        def of_layer(a):
            return pl.BlockSpec((None,) + a.shape[1:], lambda b, j: (layer, 0, 0),
                                pipeline_mode=pl.Buffered(1))

        return pl.pallas_call(
            kern,
            grid=(batch, tpb),
            in_specs=[row, full(g2), of_layer(wup), full(cw8), full(cb8), of_layer(wdown)],
            out_specs=[row, pl.BlockSpec((SUBLANES, f2), lambda b, j: (b * tpb + j, 0))],
            out_shape=[jax.ShapeDtypeStruct((n, d), F32),
                       jax.ShapeDtypeStruct((n // tm * SUBLANES, f2), F32)],
            scratch_shapes=[pltpu.VMEM((SUBLANES, f2), F32),
                            pltpu.VMEM((4, tm + SUBLANES, FF_CHUNK), F32),
                            pltpu.VMEM((tm, D_FF), BF16)],
            compiler_params=_cparams("parallel", "arbitrary"),
            name="conv_ffn_long",
        )(x, g2, wup, cw8, cb8, wdown)
    row = pl.BlockSpec((tm, d), lambda i: (i, 0))
    hrow = pl.BlockSpec((tm, f2), lambda i: (i, 0))

    def full(a):
        return pl.BlockSpec(a.shape, lambda i: (0,) * a.ndim)

    def of_layer(a):
        return pl.BlockSpec((None,) + a.shape[1:], lambda i: (layer, 0, 0))

    return pl.pallas_call(
        kern,
        grid=(n // tm,),
        in_specs=[row, full(g2), of_layer(wup), full(cw8), full(cb8), of_layer(wdown), hrow],
        out_specs=[row, hrow],
        out_shape=[jax.ShapeDtypeStruct((n, d), F32), jax.ShapeDtypeStruct((n, f2), F32)],
        compiler_params=_cparams("parallel"),
        name="conv_ffn_short",
    )(x, g2, wup, cw8, cb8, wdown, prev)


def _qkv_kernel(*refs, long_mode, tm, seq):
    if long_mode:
        x_ref, g_ref, w_ref, ct_ref, s1_ref, s2_ref = refs[:6]
        a_refs = refs[6:6 + N_GROUPS]
        kv_refs = refs[6 + N_GROUPS:6 + 2 * N_GROUPS]
        scr = refs[6 + 2 * N_GROUPS:]
    else:
        x_ref, g_ref, w_ref, ct_ref, s1_ref, s2_ref, o_ref = refs
    hn = _rms(x_ref[...], g_ref[...]).astype(BF16)
    ct = ct_ref[...]
    s1 = s1_ref[...]
    s2 = s2_ref[...]
    per_group = ATT_WIDTH // LANES
    per_part = N_GROUPS * per_group
    for c in range(3 * per_part):
        cs = slice(c * LANES, (c + 1) * LANES)
        if c % 2 == 0:
            xb2 = _dot(hn, w_ref[:, c * LANES:(c + 2) * LANES])
        xb = xb2[:, (c % 2) * LANES:(c % 2 + 1) * LANES]
        part, rem = divmod(c, per_part)
        if part < 2:
            xb = (xb * ct + pltpu.roll(xb, LANES - ROT_DIM // 2, axis=1) * s1
                  + pltpu.roll(xb, ROT_DIM // 2, axis=1) * s2)
        if long_mode:
            gi, cb = divmod(rem, per_group)
            scr[gi][part * per_group + cb] = xb * ATT_SCALE if part == 0 else xb
        else:
            o_ref[:, cs] = xb
    if long_mode:
        for gi in range(N_GROUPS):
            window, dil = ATT_GROUPS[gi]
            kb = min(min(window, seq), tm)
            for cb in range(3 * per_group):
                cs = slice(cb * LANES, (cb + 1) * LANES)
                for res in range(dil):
                    a_refs[gi][0, res, :, cs] = (
                        scr[gi][cb, pl.ds(res, tm // dil, stride=dil), :].astype(BF16))
                if cb >= per_group:
                    kv_refs[gi][0, :, (cb - per_group) * LANES:(cb - per_group + 1) * LANES] = (
                        scr[gi][cb, tm - kb:tm, :])


def _rope_tables(pos):
    half = ROT_DIM // 2
    inv = np.power(ROPE_THETA, -np.arange(half, dtype=np.float64) / half)
    ang = np.asarray(pos, np.float64)[:, None] * inv[None, :]
    cos, sin = np.cos(ang), np.sin(ang)
    n = len(pos)
    one = np.ones((n, ATT_HEAD_DIM - ROT_DIM))
    zero = np.zeros((n, ATT_HEAD_DIM - ROT_DIM))
    zh = np.zeros((n, half))
    ct = np.concatenate([cos, cos, one], axis=1)
    s1 = np.concatenate([-sin, zh, zero], axis=1)
    s2 = np.concatenate([zh, sin, zero], axis=1)
    rep = LANES // ATT_HEAD_DIM
    return tuple(np.tile(a, (1, rep)).astype(np.float32) for a in (ct, s1, s2))


def _qkv_call(x, g, wqkv, tables, tm, batch=None):
    n, d = x.shape
    nq = wqkv.shape[1]
    gspec_shape = (1, d)
    if batch is None:
        row = pl.BlockSpec((tm, d), lambda i: (i, 0))
        tab = pl.BlockSpec((tm, LANES), lambda i: (0, 0))
        return pl.pallas_call(
            functools.partial(_qkv_kernel, long_mode=False, tm=tm, seq=None),
            grid=(n // tm,),
            in_specs=[row, pl.BlockSpec(gspec_shape, lambda i: (0, 0)),
                      pl.BlockSpec(wqkv.shape, lambda i: (0, 0)), tab, tab, tab],
            out_specs=pl.BlockSpec((tm, nq), lambda i: (i, 0)),
            out_shape=jax.ShapeDtypeStruct((n, nq), F32),
            compiler_params=_cparams("parallel"),
            name="attn_qkv_rope_short",
        )(x, g.reshape(1, d), wqkv, *tables)
    s = n // batch
    tpb = s // tm
    row = pl.BlockSpec((tm, d), lambda b, j: (b * tpb + j, 0))
    tab = pl.BlockSpec((tm, LANES), lambda b, j: (j, 0))
    a_specs, a_shapes, kv_specs, kv_shapes = [], [], [], []
    for window, dil in ATT_GROUPS:
        a_specs.append(pl.BlockSpec((1, dil, tm // dil, 3 * ATT_WIDTH), lambda b, j: (b, 0, j, 0)))
        a_shapes.append(jax.ShapeDtypeStruct((batch, dil, s // dil, 3 * ATT_WIDTH), BF16))
        keep = min(window, s)
        kb = min(keep, tm)
        kv_specs.append(pl.BlockSpec(
            (1, kb, 2 * ATT_WIDTH),
            lambda b, j, kb=kb, keep=keep: (b, jnp.maximum((j + 1) * tm - kb - (s - keep), 0) // kb, 0)))
        kv_shapes.append(jax.ShapeDtypeStruct((batch, keep, 2 * ATT_WIDTH), F32))
    outs = pl.pallas_call(
        functools.partial(_qkv_kernel, long_mode=True, tm=tm, seq=s),
        grid=(batch, tpb),
        in_specs=[row, pl.BlockSpec(gspec_shape, lambda b, j: (0, 0)),
                  pl.BlockSpec(wqkv.shape, lambda b, j: (0, 0), pipeline_mode=pl.Buffered(1)),
                  tab, tab, tab],
        out_specs=a_specs + kv_specs,
        out_shape=a_shapes + kv_shapes,
        scratch_shapes=[pltpu.VMEM((3 * ATT_WIDTH // LANES, tm, LANES), F32)] * N_GROUPS,
        compiler_params=_cparams("parallel", "arbitrary"),
        name="attn_qkv_rope_long",
    )(x, g.reshape(1, d), wqkv, *tables)
    return outs[:N_GROUPS], outs[N_GROUPS:]


def _attn_prompt_kernel(q_ref, kp_ref, kc_ref, vp_ref, vc_ref, o_ref, l_ref, *, nr, nq):
    blk = ATT_BLK
    qi = lax.broadcasted_iota(jnp.int32, (blk, 2 * blk), 0)
    kj = lax.broadcasted_iota(jnp.int32, (blk, 2 * blk), 1)
    dist = qi + blk - kj
    band = (dist >= 0) & (dist <= blk)
    first = band & (pl.program_id(2) * nq * blk + kj - blk >= 0)
    low = lax.broadcasted_iota(jnp.int32, (blk, LANES), 1) < ATT_HEAD_DIM
    nt = (((1,), (1,)), ((), ()))
    for r in range(nr):
        for sub in range(nq):
            rows = slice(sub * blk, (sub + 1) * blk)
            prows = slice((sub - 1) * blk, sub * blk)
            valid = first if sub == 0 else band
            q = q_ref[0, r, rows, :]
            for hp in range(ATT_WIDTH // LANES):
                cs = slice(hp * LANES, (hp + 1) * LANES)
                kprev = kp_ref[0, r, :, cs] if sub == 0 else kc_ref[0, r, prows, cs]
                vprev = vp_ref[0, r, :, cs] if sub == 0 else vc_ref[0, r, prows, cs]
                kp = jnp.concatenate([kprev, kc_ref[0, r, rows, cs]], axis=0)
                vp = jnp.concatenate([vprev, vc_ref[0, r, rows, cs]], axis=0)
                qp = q[:, cs]
                outs = []
                for par in range(2):
                    qm = jnp.where(low if par == 0 else jnp.logical_not(low), qp,
                                   jnp.zeros_like(qp))
                    s = lax.dot_general(qm, kp, nt, preferred_element_type=F32)
                    s = jnp.where(valid, s, -jnp.inf)
                    m = jnp.max(s, axis=-1, keepdims=True)
                    p = jnp.exp(s - m)
                    l = jnp.sum(p, axis=-1, keepdims=True)
                    outs.append((_dot(p.astype(BF16), vp) / l,
                                 jnp.broadcast_to(m + jnp.log(l), (blk, LANES))))
                o_ref[0, r, rows, cs] = jnp.where(low, outs[0][0], outs[1][0]).astype(BF16)
                l_ref[0, r, rows, cs] = jnp.where(low, outs[0][1], outs[1][1])


def _attn_prompt_call(a, gi):
    b, dil, n_sub, _ = a.shape
    nb = n_sub // ATT_BLK
    nq = min(nb, ATT_STEP_BLOCKS)
    nr = min(dil, ATT_STEP_BLOCKS // nq)
    curshape = (1, nr, nq * ATT_BLK, ATT_WIDTH)

    def cur(part):
        return pl.BlockSpec(curshape, lambda i, r, n: (i, r, n, part))

    def prv(part):
        return pl.BlockSpec((1, nr, ATT_BLK, ATT_WIDTH),
                            lambda i, r, n: (i, r, jnp.maximum(n * nq - 1, 0), part))

    return pl.pallas_call(
        functools.partial(_attn_prompt_kernel, nr=nr, nq=nq),
        grid=(b, dil // nr, nb // nq),
        in_specs=[cur(0), prv(1), cur(1), prv(2), cur(2)],
        out_specs=[cur(0), cur(0)],
        out_shape=[jax.ShapeDtypeStruct((b, dil, n_sub, ATT_WIDTH), BF16),
                   jax.ShapeDtypeStruct((b, dil, n_sub, ATT_WIDTH), F32)],
        compiler_params=_cparams("parallel", "parallel", "parallel"),
        name=f"attn_prompt_g{gi}",
    )(a, a, a, a, a)


def _attn_sample_kernel(qkv_ref, c0_ref, c1_ref, c2_ref, o_ref, *, t):
    caches = (c0_ref, c1_ref, c2_ref)
    nrow = ATT_HEADS * t
    rowi = lax.broadcasted_iota(jnp.int32, (nrow, ATT_WIDTH), 0)
    coli = lax.broadcasted_iota(jnp.int32, (nrow, ATT_WIDTH), 1)
    headmask = (rowi // t) == (coli // ATT_HEAD_DIM)
    pad = jnp.zeros((LANES - t, ATT_WIDTH), F32)
    nt = (((1,), (1,)), ((), ()))
    stats = []
    for gi in range(N_GROUPS):
        window, dil = ATT_GROUPS[gi]
        cref = caches[gi]
        ln = cref.shape[-1]
        q = qkv_ref[0, :, gi * ATT_WIDTH:(gi + 1) * ATT_WIDTH] * ATT_SCALE
        knew = qkv_ref[0, :, (N_GROUPS + gi) * ATT_WIDTH:(N_GROUPS + gi + 1) * ATT_WIDTH]
        vnew = qkv_ref[0, :, (2 * N_GROUPS + gi) * ATT_WIDTH:(2 * N_GROUPS + gi + 1) * ATT_WIDTH]
        qexp = jnp.where(headmask, jnp.concatenate([q] * ATT_HEADS, axis=0), 0.0).astype(BF16)
        kn = jnp.concatenate([knew, pad], axis=0).astype(BF16)
        vn = jnp.concatenate([vnew, pad], axis=0).astype(BF16)
        sc = _dot(qexp, cref[0, 0].astype(BF16))
        sn = lax.dot_general(qexp, kn, nt, preferred_element_type=F32)
        qi_c = lax.broadcasted_iota(jnp.int32, (nrow, ln), 0) % t
        pos_c = lax.broadcasted_iota(jnp.int32, (nrow, ln), 1)
        delta_c = ln + qi_c - pos_c
        ok_c = (delta_c <= window) & (delta_c % dil == 0)
        qi_n = lax.broadcasted_iota(jnp.int32, (nrow, LANES), 0) % t
        j_n = lax.broadcasted_iota(jnp.int32, (nrow, LANES), 1)
        delta_n = qi_n - j_n
        ok_n = (j_n < t) & (delta_n >= 0) & (delta_n <= window) & (delta_n % dil == 0)
        sc = jnp.where(ok_c, sc, -jnp.inf)
        sn = jnp.where(ok_n, sn, -jnp.inf)
        m = jnp.maximum(jnp.max(sc, axis=-1, keepdims=True), jnp.max(sn, axis=-1, keepdims=True))
        pc = jnp.exp(sc - m)
        pn = jnp.exp(sn - m)
        l = jnp.sum(pc, axis=-1, keepdims=True) + jnp.sum(pn, axis=-1, keepdims=True)
        og = (lax.dot_general(pc.astype(BF16), cref[0, 1].astype(BF16), nt,
                              preferred_element_type=F32)
              + _dot(pn.astype(BF16), vn))
        stats.append((m + jnp.log(l), l, og))

    mx = functools.reduce(jnp.maximum, [s[0] for s in stats])
    es = [jnp.exp(s[0] - mx) for s in stats]
    den = es[0] + es[1] + es[2]
    full = sum(s[2] * (e / (den * s[1])) for s, e in zip(stats, es))
    for h in range(ATT_HEADS):
        hs = slice(h * ATT_HEAD_DIM, (h + 1) * ATT_HEAD_DIM)
        o_ref[0, :, hs] = full[h * t:(h + 1) * t, hs]


def _attn_sample_call(qkv, caches, b, t):
    nq = qkv.shape[-1]
    views = [c.transpose(0, 2, 3, 4, 1).reshape(b, 2, ATT_WIDTH, c.shape[1]) for c in caches]
    return pl.pallas_call(
        functools.partial(_attn_sample_kernel, t=t),
        grid=(b,),
        in_specs=[pl.BlockSpec((1, t, nq), lambda i: (i, 0, 0))]
                 + [pl.BlockSpec((1,) + v.shape[1:], lambda i: (i, 0, 0, 0)) for v in views],
        out_specs=pl.BlockSpec((1, t, ATT_WIDTH), lambda i: (i, 0, 0)),
        out_shape=jax.ShapeDtypeStruct((b, t, ATT_WIDTH), F32),
        compiler_params=_cparams("parallel"),
        name="attn_sample",
    )(qkv.reshape(b, t, nq), *views)


def _attn_out_kernel(*refs, merge, tm):
    if merge:
        o_refs = refs[:N_GROUPS]
        l_refs = refs[N_GROUPS:2 * N_GROUPS]
        x_ref, g_ref, wo_ref, out_ref = refs[2 * N_GROUPS:2 * N_GROUPS + 4]
        scr = refs[2 * N_GROUPS + 4:]
        vals = []
        for gi, ref in enumerate(o_refs + l_refs):
            dil = ATT_GROUPS[gi % N_GROUPS][1]
            if dil == 1:
                vals.append(ref[0, 0])
            else:
                for cb in range(ATT_WIDTH // LANES):
                    for res in range(dil):
                        scr[gi][cb, pl.ds(res, tm // dil, stride=dil), :] = (
                            ref[0, res, :, cb * LANES:(cb + 1) * LANES].astype(F32))
                vals.append(jnp.concatenate([scr[gi][cb] for cb in range(ATT_WIDTH // LANES)], axis=1))
        oa, ob, oc, la, lb, lc = vals
        mx = jnp.maximum(jnp.maximum(la, lb), lc)
        ea, eb, ec = jnp.exp(la - mx), jnp.exp(lb - mx), jnp.exp(lc - mx)
        den = ea + eb + ec
        o = (ea / den) * oa + (eb / den) * ob + (ec / den) * oc
    else:
        o0, x_ref, g_ref, wo_ref, out_ref = refs
        o = o0[...]
    y = _dot(o.astype(BF16), wo_ref[...])
    out_ref[...] = x_ref[...] + _rms(y, g_ref[...])


def _attn_out_call(os_, lses, x, g, wo, tm, batch=None):
    n, d = x.shape
    merge = lses is not None
    kern = functools.partial(_attn_out_kernel, merge=merge, tm=tm)
    if not merge:
        row = pl.BlockSpec((tm, d), lambda i: (i, 0))
        return pl.pallas_call(
            kern,
            grid=(n // tm,),
            in_specs=[pl.BlockSpec((tm, ATT_WIDTH), lambda i: (i, 0)), row,
                      pl.BlockSpec((1, d), lambda i: (0, 0)), pl.BlockSpec(wo.shape, lambda i: (0, 0))],
            out_specs=row,
            out_shape=jax.ShapeDtypeStruct((n, d), F32),
            compiler_params=_cparams("parallel"),
            name="attn_out_short",
        )(os_[0], x, g.reshape(1, d), wo)
    tpb = n // batch // tm
    row = pl.BlockSpec((tm, d), lambda b, j: (b * tpb + j, 0))
    aspecs = [pl.BlockSpec((1, dil, tm // dil, ATT_WIDTH), lambda b, j: (b, 0, j, 0))
              for _, dil in ATT_GROUPS]
    return pl.pallas_call(
        kern,
        grid=(batch, tpb),
        in_specs=aspecs * 2 + [row, pl.BlockSpec((1, d), lambda b, j: (0, 0)),
                               pl.BlockSpec(wo.shape, lambda b, j: (0, 0))],
        out_specs=row,
        out_shape=jax.ShapeDtypeStruct((n, d), F32),
        scratch_shapes=[pltpu.VMEM((ATT_WIDTH // LANES, tm, LANES), F32)] * (2 * N_GROUPS),
        compiler_params=_cparams("parallel", "parallel"),
        name="attn_out_long",
    )(*os_, *lses, x, g.reshape(1, d), wo)


def _pad_rows8(a):
    return jnp.concatenate([a, jnp.zeros((SUBLANES - a.shape[0],) + a.shape[1:], a.dtype)], axis=0)


def _pad_to(a, axis, size):
    padw = [(0, 0)] * a.ndim
    padw[axis] = (0, size - a.shape[axis])
    return jnp.pad(a, padw)


def _prep_weights(norm_g, rw, at_wqkv, at_wo, ff):
    (mu, wr, wk, wv, wo, w0, w1, w2, a0, a1, a2, g1, g2, k_k, k_a, r_k, lnx_g, lnx_b) = [p[0] for p in rw]
    glora = 2 * LANES
    sc = lambda a: _to_scan_heads(a.astype(BF16))
    proj_ws = [sc(wr), sc(wk), sc(wv),
               _pad_to(w1, 1, LANES).astype(BF16), _pad_to(sc(w2), 0, LANES),
               _pad_to(a1, 1, LANES).astype(BF16), _pad_to(sc(a2), 0, LANES),
               _pad_to(g1, 1, glora).astype(BF16), _pad_to(g2, 0, glora).astype(BF16)]
    return dict(
        mu8=_pad_rows8(mu),
        proj_vec=_pad_rows8(_to_scan_heads(jnp.stack([w0, a0, k_k, k_a, r_k.reshape(-1)]))),
        proj_ws=proj_ws,
        out_vec=_pad_rows8(jnp.stack([lnx_g, lnx_b, norm_g[0, 1]])),
        rw_wo=wo.astype(BF16),
        wqkv=at_wqkv[0].astype(BF16),
        at_wo=at_wo[0].astype(BF16),
        ff=[(_pad_rows8(jnp.stack([norm_g[l, 2], norm_g[l, 3]])),
             _pad_rows8(ff[1][l]), _pad_rows8(ff[2][l][None])) for l in range(2)],
        ff_wup=ff[0].astype(BF16),
        ff_wdown=ff[3].astype(BF16),
        ones=_seg_ones(),
    )


def _run_group(x, pos, shift0, wkv0, kv_bufs, conv0, norm_g, w, tm):
    b, t, d = x.shape
    n = b * t
    tm = min(tm, n)
    long_mode = shift0 is None
    xf = x.reshape(n, d)
    gn0 = norm_g[0, 0].reshape(1, d)
    to3 = lambda a: a.reshape(b, t, d)

    if long_mode:
        r, dec, k, kk, ka, g, bonus, v, hl = _rwkv_proj_call(xf, gn0, w, min(tm, 256), batch=b)
        new_shift = hl[:, SUBLANES - 1][None]
        s0 = jnp.zeros((b, SCAN_ROWS, LANES), F32)
    else:
        sh = jnp.concatenate([shift0[0][:, None, :], jnp.zeros((b, t - 1, d), F32)], axis=1)
        r, dec, k, kk, ka, g, bonus, v, hn = _rwkv_proj_call(
            xf, gn0, w, tm, seq_len=t, shift_rows=sh.reshape(n, d))
        new_shift = to3(hn)[:, -1][None]
        s0 = _state_to_scan(wkv0[0])
    y, s_fin = _scan_call(to3(kk), to3(dec), to3(k), to3(ka), to3(r), to3(v), s0,
                          min(t, SCAN_CHUNK))
    x1 = _rwkv_out_call(y.reshape(n, d), bonus, g, xf, w["out_vec"], w["rw_wo"], w["ones"], tm)
    new_wkv = _state_from_scan(s_fin)[None]

    def ffn(xin, layer):
        g2, cw8, cb8 = w["ff"][layer]
        wup, wdown = w["ff_wup"], w["ff_wdown"]
        if long_mode:
            ftm = min(t, FF_LONG_TILE)
            xo, hl = _ffn_call(xin, g2, wup, cw8, cb8, wdown, ftm, layer, batch=b)
            cs = hl.reshape(b, t // ftm, SUBLANES, 2 * D_FF)[:, -1, SUBLANES - (CONV_W - 1):]
        else:
            st = conv0[layer]
            prev = _pad_to(st, 1, t).reshape(n, 2 * D_FF)
            xo, hl = _ffn_call(xin, g2, wup, cw8, cb8, wdown, tm, layer, seq_len=t, prev=prev)
            cs = hl.reshape(b, t, 2 * D_FF)[:, t - (CONV_W - 1):]
        return xo, cs

    x2, conv_a = ffn(x1, 0)

    new_kv = []
    if long_mode:
        a_g, kv_g = _qkv_call(x2, norm_g[1, 0], w["wqkv"], _rope_tables(pos), tm, batch=b)
        for kv in kv_g:
            new_kv.append(kv.reshape(1, b, kv.shape[1], 2, ATT_HEADS, ATT_HEAD_DIM))
        res = [_attn_prompt_call(a_g[gi], gi) for gi in range(N_GROUPS)]
        x3 = _attn_out_call([o for o, _ in res], [l for _, l in res], x2, norm_g[1, 1],
                            w["at_wo"], tm, batch=b)
    else:
        tables = tuple(np.tile(a, (tm // t, 1)) for a in _rope_tables(pos))
        qkv = _qkv_call(x2, norm_g[1, 0], w["wqkv"], tables, tm)
        kvt = qkv.reshape(b, t, 3, N_GROUPS, ATT_HEADS, ATT_HEAD_DIM)[:, :, 1:]
        kvt = kvt.transpose(3, 1, 2, 4, 5, 0)
        for gi in range(N_GROUPS):
            new_kv.append(kvt[gi].transpose(4, 0, 1, 2, 3)[None])
        o = _attn_sample_call(qkv, [buf[0] for buf in kv_bufs], b, t)
        x3 = _attn_out_call([o.reshape(n, ATT_WIDTH)], None, x2, norm_g[1, 1], w["at_wo"], tm)
    x4, conv_b = ffn(x3, 1)
    return (x4.reshape(b, t, d), new_shift, new_wkv, new_kv, jnp.stack([conv_a, conv_b]))


def kernel(x_prompt, x_sample, state_shift, state_wkv, cache_kv_w128, cache_kv_w512, cache_kv_w2048, state_conv, norm_g, rw_mu, rw_wr, rw_wk, rw_wv, rw_wo, rw_w0, rw_w1, rw_w2, rw_a0, rw_a1, rw_a2, rw_g1, rw_g2, rw_kk, rw_ka, rw_rk, rw_lnx_g, rw_lnx_b, at_wqkv, at_wo, ff_wup, ff_conv_w, ff_conv_b, ff_wdown):
    rw = (rw_mu, rw_wr, rw_wk, rw_wv, rw_wo, rw_w0, rw_w1, rw_w2, rw_a0, rw_a1, rw_a2,
          rw_g1, rw_g2, rw_kk, rw_ka, rw_rk, rw_lnx_g, rw_lnx_b)
    ff = (ff_wup, ff_conv_w, ff_conv_b, ff_wdown)
    w = _prep_weights(norm_g, rw, at_wqkv, at_wo, ff)
    tp = x_prompt.shape[1]
    ts = x_sample.shape[1]
    pos_p = np.arange(tp)
    y_p, p_shift, p_wkv, p_kv, p_conv = _run_group(
        x_prompt, pos_p, None, None, None, None, norm_g, w, 512)
    pos_s = PAST_LEN + np.arange(ts)
    y_s, s_shift, s_wkv, s_kv, s_conv = _run_group(
        x_sample, pos_s, state_shift, state_wkv,
        (cache_kv_w128, cache_kv_w512, cache_kv_w2048), state_conv, norm_g, w, 256)
    return (y_p, y_s, p_shift, p_wkv, p_kv[0], p_kv[1], p_kv[2], p_conv,
            s_shift, s_wkv, s_kv[0], s_kv[1], s_kv[2], s_conv)
```

```python
import functools
import math

import jax
import jax.numpy as jnp
import numpy as np
from jax import lax
from jax.experimental import pallas as pl
from jax.experimental.pallas import tpu as pltpu

F32 = jnp.float32
BF16 = jnp.bfloat16

D_MODEL = 1024
RW_HEAD = 64
RW_HEADS = D_MODEL // RW_HEAD
RW_GN_EPS = 64e-5
ATT_GROUPS = ((128, 1), (512, 4), (2048, 16))
N_GROUPS = 3
ATT_HEADS = 8
ATT_HEAD_DIM = 64
ATT_WIDTH = ATT_HEADS * ATT_HEAD_DIM
ATT_SCALE = ATT_HEAD_DIM ** -0.5
ATT_BLK = 128
ATT_STEP_BLOCKS = 4
ROT_DIM = ATT_HEAD_DIM // 4
ROPE_THETA = 500000.0
D_FF = 2816
CONV_W = 3
NORM_EPS = 1e-6
PAST_LEN = 2048

LANES = 128
SUBLANES = 8
FF_CHUNK = 256
FF_DOWN_CHUNKS = 11
FF_LONG_TILE = 512
OUT_TILE = 1024
SCAN_CHUNK = 128
SCAN_HALF = 64
SCAN_ROWS = RW_HEADS // 2 * RW_HEAD
SCAN_NB = 2
SCAN_NB_SHORT = 4
VMEM_LIMIT = 56 * 1024 * 1024


def _cparams(*sem):
    return pltpu.CompilerParams(dimension_semantics=sem, vmem_limit_bytes=VMEM_LIMIT)


def _dot(a, b):
    return jnp.dot(a, b, preferred_element_type=F32)


def _rms(x, g):
    ms = jnp.mean(x * x, axis=-1, keepdims=True)
    return x * lax.rsqrt(ms + NORM_EPS) * g


def _segsum(x, ones):
    outs = []
    for c in range(x.shape[1] // LANES):
        xc = x[:, c * LANES:(c + 1) * LANES]
        hi = xc.astype(BF16)
        lo = (xc - hi.astype(F32)).astype(BF16)
        outs.append(_dot(hi, ones) + _dot(lo, ones))
    return jnp.concatenate(outs, axis=1)


def _scan_to_natural_heads(x):
    low = lax.broadcasted_iota(jnp.int32, (x.shape[0], LANES), 1) < RW_HEAD
    outs = []
    for c in range(0, x.shape[1] // LANES, 2):
        a = x[:, c * LANES:(c + 1) * LANES]
        b = x[:, (c + 1) * LANES:(c + 2) * LANES]
        outs.append(jnp.where(low, a, pltpu.roll(b, RW_HEAD, axis=1)))
        outs.append(jnp.where(low, pltpu.roll(a, RW_HEAD, axis=1), b))
    return jnp.concatenate(outs, axis=1)


def _rwkv_proj_kernel(*refs, seq_len, tm):
    if seq_len is None:
        (x_ref, gn_ref, mu_ref, vec_ref, wr_ref, wk_ref, wv_ref, w1_ref, w2_ref, a1_ref,
         a2_ref, g1_ref, g2_ref, ones_ref,
         r_ref, d_ref, k_ref, kk_ref, ka_ref, g_ref, bonus_ref, v_ref, hl_ref, carry) = refs
    else:
        (x_ref, gn_ref, mu_ref, vec_ref, wr_ref, wk_ref, wv_ref, w1_ref, w2_ref, a1_ref,
         a2_ref, g1_ref, g2_ref, ones_ref, sh_ref,
         r_ref, d_ref, k_ref, kk_ref, ka_ref, g_ref, bonus_ref, v_ref, hn_ref) = refs
    hn = _rms(x_ref[...], gn_ref[...])
    rolled = pltpu.roll(hn, 1, axis=0)
    rows = lax.broadcasted_iota(jnp.int32, (tm, D_MODEL), 0)
    if seq_len is None:
        @pl.when(pl.program_id(1) == 0)
        def _():
            carry[...] = jnp.zeros(carry.shape, F32)

        prev = jnp.where(rows == 0, carry[0:1, :], rolled)
        carry[0:1, :] = hn[tm - 1:tm, :]
        hl_ref[0] = hn[tm - SUBLANES:tm, :]
    else:
        prev = jnp.where(rows % seq_len == 0, sh_ref[...], rolled)
        hn_ref[...] = hn
    xx = prev - hn

    def mix(i):
        return (hn + xx * mu_ref[i:i + 1, :]).astype(BF16)

    w0 = vec_ref[0:1, :]
    a0 = vec_ref[1:2, :]
    k_k = vec_ref[2:3, :]
    k_a = vec_ref[3:4, :]
    r_k = vec_ref[4:5, :]

    r = _dot(mix(0), wr_ref[...])
    k = _dot(mix(2), wk_ref[...])
    v = _dot(mix(3), wv_ref[...])
    wl = _dot(jnp.tanh(_dot(mix(1), w1_ref[...])).astype(BF16), w2_ref[...])
    z = -(w0 + wl)
    softplus = jnp.maximum(z, 0.0) + jnp.log1p(jnp.exp(-jnp.abs(z)))
    w = -softplus - 0.5
    decay = jnp.exp(-jnp.exp(w))
    al = _dot(_dot(mix(4), a1_ref[...]).astype(BF16), a2_ref[...])
    a = jax.nn.sigmoid(a0 + al)
    g = _dot(jax.nn.sigmoid(_dot(mix(5), g1_ref[...])).astype(BF16), g2_ref[...])

    ones = ones_ref[...]
    kk = k * k_k
    nrm = jnp.sqrt(_segsum(kk * kk, ones))
    kk = kk / jnp.maximum(nrm, 1e-12)
    k2 = k * (1.0 + (a - 1.0) * k_a)

    r_ref[...] = r
    d_ref[...] = decay
    k_ref[...] = k2
    kk_ref[...] = kk
    ka_ref[...] = kk * a
    g_ref[...] = g.astype(BF16)
    bonus_ref[...] = _scan_to_natural_heads(_segsum(r * k2 * r_k, ones) * v).astype(BF16)
    v_ref[...] = v


def _rwkv_proj_call(x, gn, w, tm, batch=None, seq_len=None, shift_rows=None):
    n, d = x.shape
    nd = [jax.ShapeDtypeStruct((n, d), F32)]
    ndh = [jax.ShapeDtypeStruct((n, d), BF16)]
    ws = w["proj_ws"]
    if seq_len is None:
        t = n // batch
        tpb = t // tm
        row = pl.BlockSpec((tm, d), lambda b, j: (b * tpb + j, 0))

        def full(a):
            return pl.BlockSpec(a.shape, lambda b, j: (0,) * a.ndim, pipeline_mode=pl.Buffered(1))

        ins = [x, gn, w["mu8"], w["proj_vec"]] + ws + [w["ones"]]
        return pl.pallas_call(
            functools.partial(_rwkv_proj_kernel, seq_len=None, tm=tm),
            grid=(batch, tpb),
            in_specs=[row] + [full(a) for a in ins[1:]],
            out_specs=[row] * 8 + [pl.BlockSpec((1, SUBLANES, d), lambda b, j: (b, 0, 0))],
            out_shape=nd * 5 + ndh * 2 + nd + [jax.ShapeDtypeStruct((batch, SUBLANES, d), F32)],
            scratch_shapes=[pltpu.VMEM((SUBLANES, d), F32)],
            compiler_params=_cparams("parallel", "arbitrary"),
            name="rwkv_proj_long",
        )(*ins)
    row = pl.BlockSpec((tm, d), lambda i: (i, 0))

    def full(a):
        return pl.BlockSpec(a.shape, lambda i: (0,) * a.ndim)

    ins = [x, gn, w["mu8"], w["proj_vec"]] + ws + [w["ones"]]
    return pl.pallas_call(
        functools.partial(_rwkv_proj_kernel, seq_len=seq_len, tm=tm),
        grid=(n // tm,),
        in_specs=[row] + [full(a) for a in ins[1:]] + [row],
        out_specs=[row] * 9,
        out_shape=nd * 5 + ndh * 2 + nd * 2,
        compiler_params=_cparams("parallel"),
        name="rwkv_proj_short",
    )(*ins, shift_rows)


def _scan_kernel(kk_ref, d_ref, k_ref, ka_ref, r_ref, v_ref, s0_ref, selv_ref, place_ref,
                 ones_ref, y_ref, sout_ref, s_scr, lv_scr, yt_scr, *, steps, nb):
    c = pl.program_id(1)

    @pl.when(c == 0)
    def _():
        s_scr[...] = s0_ref[...]

    ones = ones_ref[...]
    low = lax.broadcasted_iota(jnp.int32, (SCAN_ROWS, LANES), 1) < RW_HEAD
    n_half = -(-steps // SCAN_HALF)
    for b in range(nb):
        vb = v_ref[b]
        if steps < LANES:
            vb = jnp.concatenate([vb, jnp.zeros((LANES - steps, D_MODEL), F32)], axis=0)
        tr = [vb[:, hp * LANES:(hp + 1) * LANES].T for hp in range(RW_HEADS // 2)]
        ev = jnp.concatenate([x[0:RW_HEAD] for x in tr], axis=0)
        od = jnp.concatenate([x[RW_HEAD:2 * RW_HEAD] for x in tr], axis=0)
        lv_scr[b, 0] = jnp.where(low, ev, pltpu.roll(od, RW_HEAD, axis=1)).astype(BF16)
        if n_half > 1:
            lv_scr[b, 1] = jnp.where(low, pltpu.roll(ev, RW_HEAD, axis=1), od).astype(BF16)

    def rows(blk, j):
        return jnp.concatenate(
            [jnp.broadcast_to(blk[j:j + 1, hp * LANES:(hp + 1) * LANES], (RW_HEAD, LANES))
             for hp in range(RW_HEADS // 2)], axis=0)

    for half in range(n_half):
        nst = min(SCAN_HALF, steps - half * SCAN_HALF)
        yt_scr[...] = jnp.zeros(yt_scr.shape, F32)

        def body(grp, carry, half=half):
            base = pl.multiple_of(half * SCAN_HALF + grp * SUBLANES, SUBLANES)
            blks = [[ref[b, pl.ds(base, SUBLANES), :]
                     for ref in (kk_ref, d_ref, k_ref, ka_ref, r_ref)] for b in range(nb)]
            p2 = [[] for _ in range(nb)]
            for j in range(SUBLANES):
                for b in range(nb):
                    kk8, d8, k8, ka8, r8 = blks[b]
                    s = s_scr[b]
                    sk = _dot((s * rows(kk8, j)).astype(BF16), ones)
                    vcol = _dot(lv_scr[b, half], selv_ref[grp * SUBLANES + j])
                    s = s * rows(d8, j) - sk * rows(ka8, j) + vcol * rows(k8, j)
                    s_scr[b] = s
                    p2[b].append((s * rows(r8, j)).astype(BF16))
            prow = pl.multiple_of(grp * SUBLANES * LANES, SUBLANES * LANES)
            pl8 = place_ref[pl.ds(prow, SUBLANES * LANES), :]
            for b in range(nb):
                yt_scr[b] += _dot(jnp.concatenate(p2[b], axis=1), pl8)
            return carry

        lax.fori_loop(0, nst // SUBLANES, body, 0)

        t0 = half * SCAN_HALF
        for b in range(nb):
            yt = yt_scr[b]
            for q in range(RW_HEADS // 4):
                tr = yt[q * 2 * RW_HEAD:(q + 1) * 2 * RW_HEAD, :].T
                c0 = q * 2 * LANES
                y_ref[b, t0:t0 + nst, c0:c0 + LANES] = tr[0:nst]
                y_ref[b, t0:t0 + nst, c0 + LANES:c0 + 2 * LANES] = tr[SCAN_HALF:SCAN_HALF + nst]

    @pl.when(c == pl.num_programs(1) - 1)
    def _():
        sout_ref[...] = s_scr[...]


def _scan_tables():
    col = np.arange(LANES)
    row = np.arange(LANES)
    t = np.arange(SCAN_HALF)
    selv = ((row[None, :, None] // RW_HEAD == col[None, None, :] // RW_HEAD)
            & (row[None, :, None] % RW_HEAD == t[:, None, None]))
    place = ((row[None, :, None] // RW_HEAD == col[None, None, :] // RW_HEAD)
             & (col[None, None, :] % RW_HEAD == t[:, None, None]))
    return (jnp.asarray(selv, BF16),
            jnp.asarray(place.reshape(SCAN_HALF * LANES, LANES), BF16))


def _seg_ones():
    i = np.arange(LANES)
    return jnp.asarray(i[:, None] // RW_HEAD == i[None, :] // RW_HEAD, BF16)


def _scan_call(kk, dec, k, ka, r, v, s0, steps):
    b, t, d = kk.shape
    assert steps % SUBLANES == 0 and t % steps == 0
    nb = SCAN_NB if steps == SCAN_CHUNK else SCAN_NB_SHORT
    nb = nb if b % nb == 0 else 1
    nc = t // steps
    n_half = -(-steps // SCAN_HALF)
    selv, place = _scan_tables()
    ones = _seg_ones()
    rowspec = pl.BlockSpec((nb, steps, d), lambda i, c: (i, c, 0))

    def full(a):
        return pl.BlockSpec(a.shape, lambda i, c: (0,) * a.ndim)

    sspec = pl.BlockSpec((nb, SCAN_ROWS, LANES), lambda i, c: (i, 0, 0))
    return pl.pallas_call(
        functools.partial(_scan_kernel, steps=steps, nb=nb),
        grid=(b // nb, nc),
        in_specs=[rowspec] * 6 + [sspec, full(selv), full(place), full(ones)],
        out_specs=[rowspec, sspec],
        out_shape=[jax.ShapeDtypeStruct((b, t, d), F32),
                   jax.ShapeDtypeStruct((b, SCAN_ROWS, LANES), F32)],
        scratch_shapes=[pltpu.VMEM((nb, SCAN_ROWS, LANES), F32),
                        pltpu.VMEM((nb, n_half, SCAN_ROWS, LANES), BF16),
                        pltpu.VMEM((nb, SCAN_ROWS, LANES), F32)],
        compiler_params=_cparams("parallel", "arbitrary"),
        name="rwkv_scan",
    )(kk, dec, k, ka, r, v, s0, selv, place, ones)


def _to_scan_heads(a):
    lead = a.shape[:-1]
    return a.reshape(lead + (RW_HEADS // 4, 2, 2, RW_HEAD)).swapaxes(-3, -2).reshape(
        lead + (D_MODEL,))


def _state_to_scan(s):
    b = s.shape[0]
    s = s.reshape(b, RW_HEADS // 4, 2, 2, RW_HEAD, RW_HEAD)
    return s.transpose(0, 1, 3, 4, 2, 5).reshape(b, SCAN_ROWS, LANES)


def _state_from_scan(s):
    b = s.shape[0]
    s = s.reshape(b, RW_HEADS // 4, 2, RW_HEAD, 2, RW_HEAD)
    return s.transpose(0, 1, 4, 2, 3, 5).reshape(b, RW_HEADS, RW_HEAD, RW_HEAD)


def _rwkv_out_kernel(y_ref, bonus_ref, g_ref, x_ref, vec_ref, wo_ref, ones_ref, o_ref):
    ones = ones_ref[...]
    y = y_ref[...]
    mean = _segsum(y, ones) * (1.0 / RW_HEAD)
    yc = y - mean
    var = _segsum(yc * yc, ones) * (1.0 / RW_HEAD)
    yn = yc * lax.rsqrt(var + RW_GN_EPS) * vec_ref[0:1, :] + vec_ref[1:2, :]
    z = ((yn + bonus_ref[...]) * g_ref[...]).astype(BF16)
    mo = _dot(z, wo_ref[...])
    o_ref[...] = x_ref[...] + _rms(mo, vec_ref[2:3, :])


def _rwkv_out_call(y, bonus, g, x, vec8, wo, ones, tm):
    n, d = y.shape
    row = pl.BlockSpec((tm, d), lambda i: (i, 0))

    def full(a):
        return pl.BlockSpec(a.shape, lambda i: (0,) * a.ndim)

    return pl.pallas_call(
        _rwkv_out_kernel,
        grid=(n // tm,),
        in_specs=[row] * 4 + [full(vec8), full(wo), full(ones)],
        out_specs=row,
        out_shape=jax.ShapeDtypeStruct((n, d), F32),
        compiler_params=_cparams("parallel"),
        name="rwkv_out",
    )(y, bonus, g, x, vec8, wo, ones)


def _gelu_tanh(x):
    return 0.5 * x * (1.0 + jnp.tanh(math.sqrt(2.0 / math.pi) * (x + 0.044715 * (x * x * x))))


def _ffn_kernel(*refs, seq_len, tm):
    if seq_len is None:
        (x_ref, g_ref, wup_ref, cw_ref, cb_ref, wdown_ref, xo_ref, hl_ref,
         carry, hs_scr, act_scr) = refs
    else:
        x_ref, g_ref, wup_ref, cw_ref, cb_ref, wdown_ref, prev_ref, xo_ref, hl_ref = refs
    x = x_ref[...]
    hn = _rms(x, g_ref[0:1, :]).astype(BF16)
    n_chunks = D_FF // FF_CHUNK
    if seq_len is None:
        @pl.when(pl.program_id(1) == 0)
        def _():
            carry[...] = jnp.zeros(carry.shape, F32)
    else:
        rows = lax.broadcasted_iota(jnp.int32, (tm, FF_CHUNK), 0)
        tpos = rows % seq_len

    acc = jnp.zeros((tm, D_MODEL), F32)
    seg0 = 0
    for j in range(n_chunks):
        conv = []
        for part in range(2):
            c0 = part * D_FF + j * FF_CHUNK
            cs = slice(c0, c0 + FF_CHUNK)
            h = _dot(hn, wup_ref[:, cs])
            if seq_len is None:
                buf = (j % 2) * 2 + part
                hs_scr[buf, 0:SUBLANES, :] = carry[:, cs]
                hs_scr[buf, SUBLANES:SUBLANES + tm, :] = h
                carry[:, cs] = h[tm - SUBLANES:tm, :]
                hl_ref[:, cs] = h[tm - SUBLANES:tm, :]
                h1 = hs_scr[buf, SUBLANES - 1:SUBLANES - 1 + tm, :]
                h2 = hs_scr[buf, SUBLANES - 2:SUBLANES - 2 + tm, :]
            else:
                pv = prev_ref[:, cs]
                h1 = jnp.where(tpos == 0, pltpu.roll(pv, tm - 1, axis=0), pltpu.roll(h, 1, axis=0))
                h2 = jnp.where(tpos < 2, pv, pltpu.roll(h, 2, axis=0))
                hl_ref[:, cs] = h
            conv.append(cb_ref[0:1, cs] + h2 * cw_ref[0:1, cs] + h1 * cw_ref[1:2, cs]
                        + h * cw_ref[2:3, cs])
        act = (_gelu_tanh(conv[0]) * conv[1]).astype(BF16)
        if seq_len is None:
            act_scr[:, j * FF_CHUNK:(j + 1) * FF_CHUNK] = act
            if (j + 1) % FF_DOWN_CHUNKS == 0 or j == n_chunks - 1:
                seg = slice(seg0 * FF_CHUNK, (j + 1) * FF_CHUNK)
                acc = acc + _dot(act_scr[:, seg], wdown_ref[seg, :])
                seg0 = j + 1
        else:
            acc = acc + _dot(act, wdown_ref[j * FF_CHUNK:(j + 1) * FF_CHUNK, :])
    xo_ref[...] = x + _rms(acc, g_ref[1:2, :])


def _ffn_call(x, g2, wup, cw8, cb8, wdown, tm, layer, seq_len=None, prev=None, batch=None):
    n, d = x.shape
    f2 = 2 * D_FF
    kern = functools.partial(_ffn_kernel, seq_len=seq_len, tm=tm)
    if seq_len is None:
        tpb = n // batch // tm
        row = pl.BlockSpec((tm, d), lambda b, j: (b * tpb + j, 0))

        def full(a):
            return pl.BlockSpec(a.shape, lambda b, j: (0,) * a.ndim, pipeline_mode=pl.Buffered(1))

        def of_layer(a):
            return pl.BlockSpec((None,) + a.shape[1:], lambda b, j: (layer, 0, 0),
                                pipeline_mode=pl.Buffered(1))

        return pl.pallas_call(
            kern,
            grid=(batch, tpb),
            in_specs=[row, full(g2), of_layer(wup), full(cw8), full(cb8), of_layer(wdown)],
            out_specs=[row, pl.BlockSpec((SUBLANES, f2), lambda b, j: (b * tpb + j, 0))],
            out_shape=[jax.ShapeDtypeStruct((n, d), F32),
                       jax.ShapeDtypeStruct((n // tm * SUBLANES, f2), F32)],
            scratch_shapes=[pltpu.VMEM((SUBLANES, f2), F32),
                            pltpu.VMEM((4, tm + SUBLANES, FF_CHUNK), F32),
                            pltpu.VMEM((tm, D_FF), BF16)],
            compiler_params=_cparams("parallel", "arbitrary"),
            name="conv_ffn_long",
        )(x, g2, wup, cw8, cb8, wdown)
    row = pl.BlockSpec((tm, d), lambda i: (i, 0))
    hrow = pl.BlockSpec((tm, f2), lambda i: (i, 0))

    def full(a):
        return pl.BlockSpec(a.shape, lambda i: (0,) * a.ndim)

    def of_layer(a):
        return pl.BlockSpec((None,) + a.shape[1:], lambda i: (layer, 0, 0))

    return pl.pallas_call(
        kern,
        grid=(n // tm,),
        in_specs=[row, full(g2), of_layer(wup), full(cw8), full(cb8), of_layer(wdown), hrow],
        out_specs=[row, hrow],
        out_shape=[jax.ShapeDtypeStruct((n, d), F32), jax.ShapeDtypeStruct((n, f2), F32)],
        compiler_params=_cparams("parallel"),
        name="conv_ffn_short",
    )(x, g2, wup, cw8, cb8, wdown, prev)


def _qkv_kernel(*refs, long_mode, tm, seq):
    if long_mode:
        x_ref, g_ref, w_ref, ct_ref, s1_ref, s2_ref = refs[:6]
        a_refs = refs[6:6 + N_GROUPS]
        kv_refs = refs[6 + N_GROUPS:6 + 2 * N_GROUPS]
        scr = refs[6 + 2 * N_GROUPS:]
    else:
        x_ref, g_ref, w_ref, ct_ref, s1_ref, s2_ref, o_ref = refs
    hn = _rms(x_ref[...], g_ref[...]).astype(BF16)
    ct = ct_ref[...]
    s1 = s1_ref[...]
    s2 = s2_ref[...]
    per_group = ATT_WIDTH // LANES
    per_part = N_GROUPS * per_group
    for c in range(3 * per_part):
        cs = slice(c * LANES, (c + 1) * LANES)
        if c % 2 == 0:
            xb2 = _dot(hn, w_ref[:, c * LANES:(c + 2) * LANES])
        xb = xb2[:, (c % 2) * LANES:(c % 2 + 1) * LANES]
        part, rem = divmod(c, per_part)
        if part < 2:
            xb = (xb * ct + pltpu.roll(xb, LANES - ROT_DIM // 2, axis=1) * s1
                  + pltpu.roll(xb, ROT_DIM // 2, axis=1) * s2)
        if long_mode:
            gi, cb = divmod(rem, per_group)
            scr[gi][part * per_group + cb] = xb * ATT_SCALE if part == 0 else xb
        else:
            o_ref[:, cs] = xb
    if long_mode:
        for gi in range(N_GROUPS):
            window, dil = ATT_GROUPS[gi]
            kb = min(min(window, seq), tm)
            for cb in range(3 * per_group):
                cs = slice(cb * LANES, (cb + 1) * LANES)
                for res in range(dil):
                    a_refs[gi][0, res, :, cs] = (
                        scr[gi][cb, pl.ds(res, tm // dil, stride=dil), :].astype(BF16))
                if cb >= per_group:
                    kv_refs[gi][0, :, (cb - per_group) * LANES:(cb - per_group + 1) * LANES] = (
                        scr[gi][cb, tm - kb:tm, :])


def _rope_tables(pos):
    half = ROT_DIM // 2
    inv = np.power(ROPE_THETA, -np.arange(half, dtype=np.float64) / half)
    ang = np.asarray(pos, np.float64)[:, None] * inv[None, :]
    cos, sin = np.cos(ang), np.sin(ang)
    n = len(pos)
    one = np.ones((n, ATT_HEAD_DIM - ROT_DIM))
    zero = np.zeros((n, ATT_HEAD_DIM - ROT_DIM))
    zh = np.zeros((n, half))
    ct = np.concatenate([cos, cos, one], axis=1)
    s1 = np.concatenate([-sin, zh, zero], axis=1)
    s2 = np.concatenate([zh, sin, zero], axis=1)
    rep = LANES // ATT_HEAD_DIM
    return tuple(np.tile(a, (1, rep)).astype(np.float32) for a in (ct, s1, s2))


def _qkv_call(x, g, wqkv, tables, tm, batch=None):
    n, d = x.shape
    nq = wqkv.shape[1]
    gspec_shape = (1, d)
    if batch is None:
        row = pl.BlockSpec((tm, d), lambda i: (i, 0))
        tab = pl.BlockSpec((tm, LANES), lambda i: (0, 0))
        return pl.pallas_call(
            functools.partial(_qkv_kernel, long_mode=False, tm=tm, seq=None),
            grid=(n // tm,),
            in_specs=[row, pl.BlockSpec(gspec_shape, lambda i: (0, 0)),
                      pl.BlockSpec(wqkv.shape, lambda i: (0, 0)), tab, tab, tab],
            out_specs=pl.BlockSpec((tm, nq), lambda i: (i, 0)),
            out_shape=jax.ShapeDtypeStruct((n, nq), F32),
            compiler_params=_cparams("parallel"),
            name="attn_qkv_rope_short",
        )(x, g.reshape(1, d), wqkv, *tables)
    s = n // batch
    tpb = s // tm
    row = pl.BlockSpec((tm, d), lambda b, j: (b * tpb + j, 0))
    tab = pl.BlockSpec((tm, LANES), lambda b, j: (j, 0))
    a_specs, a_shapes, kv_specs, kv_shapes = [], [], [], []
    for window, dil in ATT_GROUPS:
        a_specs.append(pl.BlockSpec((1, dil, tm // dil, 3 * ATT_WIDTH), lambda b, j: (b, 0, j, 0)))
        a_shapes.append(jax.ShapeDtypeStruct((batch, dil, s // dil, 3 * ATT_WIDTH), BF16))
        keep = min(window, s)
        kb = min(keep, tm)
        kv_specs.append(pl.BlockSpec(
            (1, kb, 2 * ATT_WIDTH),
            lambda b, j, kb=kb, keep=keep: (b, jnp.maximum((j + 1) * tm - kb - (s - keep), 0) // kb, 0)))
        kv_shapes.append(jax.ShapeDtypeStruct((batch, keep, 2 * ATT_WIDTH), F32))
    outs = pl.pallas_call(
        functools.partial(_qkv_kernel, long_mode=True, tm=tm, seq=s),
        grid=(batch, tpb),
        in_specs=[row, pl.BlockSpec(gspec_shape, lambda b, j: (0, 0)),
                  pl.BlockSpec(wqkv.shape, lambda b, j: (0, 0), pipeline_mode=pl.Buffered(1)),
                  tab, tab, tab],
        out_specs=a_specs + kv_specs,
        out_shape=a_shapes + kv_shapes,
        scratch_shapes=[pltpu.VMEM((3 * ATT_WIDTH // LANES, tm, LANES), F32)] * N_GROUPS,
        compiler_params=_cparams("parallel", "arbitrary"),
        name="attn_qkv_rope_long",
    )(x, g.reshape(1, d), wqkv, *tables)
    return outs[:N_GROUPS], outs[N_GROUPS:]


def _attn_prompt_kernel(q_ref, kp_ref, kc_ref, vp_ref, vc_ref, o_ref, l_ref, *, nr, nq):
    blk = ATT_BLK
    qi = lax.broadcasted_iota(jnp.int32, (blk, 2 * blk), 0)
    kj = lax.broadcasted_iota(jnp.int32, (blk, 2 * blk), 1)
    dist = qi + blk - kj
    band = (dist >= 0) & (dist <= blk)
    first = band & (pl.program_id(2) * nq * blk + kj - blk >= 0)
    low = lax.broadcasted_iota(jnp.int32, (blk, LANES), 1) < ATT_HEAD_DIM
    nt = (((1,), (1,)), ((), ()))
    for r in range(nr):
        for sub in range(nq):
            rows = slice(sub * blk, (sub + 1) * blk)
            prows = slice((sub - 1) * blk, sub * blk)
            valid = first if sub == 0 else band
            q = q_ref[0, r, rows, :]
            for hp in range(ATT_WIDTH // LANES):
                cs = slice(hp * LANES, (hp + 1) * LANES)
                kprev = kp_ref[0, r, :, cs] if sub == 0 else kc_ref[0, r, prows, cs]
                vprev = vp_ref[0, r, :, cs] if sub == 0 else vc_ref[0, r, prows, cs]
                kp = jnp.concatenate([kprev, kc_ref[0, r, rows, cs]], axis=0)
                vp = jnp.concatenate([vprev, vc_ref[0, r, rows, cs]], axis=0)
                qp = q[:, cs]
                outs = []
                for par in range(2):
                    qm = jnp.where(low if par == 0 else jnp.logical_not(low), qp,
                                   jnp.zeros_like(qp))
                    s = lax.dot_general(qm, kp, nt, preferred_element_type=F32)
                    s = jnp.where(valid, s, -jnp.inf)
                    m = jnp.max(s, axis=-1, keepdims=True)
                    p = jnp.exp(s - m)
                    l = jnp.sum(p, axis=-1, keepdims=True)
                    outs.append((_dot(p.astype(BF16), vp) / l,
                                 jnp.broadcast_to(m + jnp.log(l), (blk, LANES))))
                o_ref[0, r, rows, cs] = jnp.where(low, outs[0][0], outs[1][0]).astype(BF16)
                l_ref[0, r, rows, cs] = jnp.where(low, outs[0][1], outs[1][1])


def _attn_prompt_call(a, gi):
    b, dil, n_sub, _ = a.shape
    nb = n_sub // ATT_BLK
    nq = min(nb, ATT_STEP_BLOCKS)
    nr = min(dil, ATT_STEP_BLOCKS // nq)
    curshape = (1, nr, nq * ATT_BLK, ATT_WIDTH)

    def cur(part):
        return pl.BlockSpec(curshape, lambda i, r, n: (i, r, n, part))

    def prv(part):
        return pl.BlockSpec((1, nr, ATT_BLK, ATT_WIDTH),
                            lambda i, r, n: (i, r, jnp.maximum(n * nq - 1, 0), part))

    return pl.pallas_call(
        functools.partial(_attn_prompt_kernel, nr=nr, nq=nq),
        grid=(b, dil // nr, nb // nq),
        in_specs=[cur(0), prv(1), cur(1), prv(2), cur(2)],
        out_specs=[cur(0), cur(0)],
        out_shape=[jax.ShapeDtypeStruct((b, dil, n_sub, ATT_WIDTH), BF16),
                   jax.ShapeDtypeStruct((b, dil, n_sub, ATT_WIDTH), F32)],
        compiler_params=_cparams("parallel", "parallel", "parallel"),
        name=f"attn_prompt_g{gi}",
    )(a, a, a, a, a)


def _attn_sample_kernel(qkv_ref, c0_ref, c1_ref, c2_ref, o_ref, *, t):
    caches = (c0_ref, c1_ref, c2_ref)
    nrow = ATT_HEADS * t
    rowi = lax.broadcasted_iota(jnp.int32, (nrow, ATT_WIDTH), 0)
    coli = lax.broadcasted_iota(jnp.int32, (nrow, ATT_WIDTH), 1)
    headmask = (rowi // t) == (coli // ATT_HEAD_DIM)
    pad = jnp.zeros((LANES - t, ATT_WIDTH), F32)
    nt = (((1,), (1,)), ((), ()))
    stats = []
    for gi in range(N_GROUPS):
        window, dil = ATT_GROUPS[gi]
        cref = caches[gi]
        ln = cref.shape[-1]
        q = qkv_ref[0, :, gi * ATT_WIDTH:(gi + 1) * ATT_WIDTH] * ATT_SCALE
        knew = qkv_ref[0, :, (N_GROUPS + gi) * ATT_WIDTH:(N_GROUPS + gi + 1) * ATT_WIDTH]
        vnew = qkv_ref[0, :, (2 * N_GROUPS + gi) * ATT_WIDTH:(2 * N_GROUPS + gi + 1) * ATT_WIDTH]
        qexp = jnp.where(headmask, jnp.concatenate([q] * ATT_HEADS, axis=0), 0.0).astype(BF16)
        kn = jnp.concatenate([knew, pad], axis=0).astype(BF16)
        vn = jnp.concatenate([vnew, pad], axis=0).astype(BF16)
        sc = _dot(qexp, cref[0, 0].astype(BF16))
        sn = lax.dot_general(qexp, kn, nt, preferred_element_type=F32)
        qi_c = lax.broadcasted_iota(jnp.int32, (nrow, ln), 0) % t
        pos_c = lax.broadcasted_iota(jnp.int32, (nrow, ln), 1)
        delta_c = ln + qi_c - pos_c
        ok_c = (delta_c <= window) & (delta_c % dil == 0)
        qi_n = lax.broadcasted_iota(jnp.int32, (nrow, LANES), 0) % t
        j_n = lax.broadcasted_iota(jnp.int32, (nrow, LANES), 1)
        delta_n = qi_n - j_n
        ok_n = (j_n < t) & (delta_n >= 0) & (delta_n <= window) & (delta_n % dil == 0)
        sc = jnp.where(ok_c, sc, -jnp.inf)
        sn = jnp.where(ok_n, sn, -jnp.inf)
        m = jnp.maximum(jnp.max(sc, axis=-1, keepdims=True), jnp.max(sn, axis=-1, keepdims=True))
        pc = jnp.exp(sc - m)
        pn = jnp.exp(sn - m)
        l = jnp.sum(pc, axis=-1, keepdims=True) + jnp.sum(pn, axis=-1, keepdims=True)
        og = (lax.dot_general(pc.astype(BF16), cref[0, 1].astype(BF16), nt,
                              preferred_element_type=F32)
              + _dot(pn.astype(BF16), vn))
        stats.append((m + jnp.log(l), l, og))

    mx = functools.reduce(jnp.maximum, [s[0] for s in stats])
    es = [jnp.exp(s[0] - mx) for s in stats]
    den = es[0] + es[1] + es[2]
    full = sum(s[2] * (e / (den * s[1])) for s, e in zip(stats, es))
    for h in range(ATT_HEADS):
        hs = slice(h * ATT_HEAD_DIM, (h + 1) * ATT_HEAD_DIM)
        o_ref[0, :, hs] = full[h * t:(h + 1) * t, hs]


def _attn_sample_call(qkv, caches, b, t):
    nq = qkv.shape[-1]
    views = [c.transpose(0, 2, 3, 4, 1).reshape(b, 2, ATT_WIDTH, c.shape[1]) for c in caches]
    return pl.pallas_call(
        functools.partial(_attn_sample_kernel, t=t),
        grid=(b,),
        in_specs=[pl.BlockSpec((1, t, nq), lambda i: (i, 0, 0))]
                 + [pl.BlockSpec((1,) + v.shape[1:], lambda i: (i, 0, 0, 0)) for v in views],
        out_specs=pl.BlockSpec((1, t, ATT_WIDTH), lambda i: (i, 0, 0)),
        out_shape=jax.ShapeDtypeStruct((b, t, ATT_WIDTH), F32),
        compiler_params=_cparams("parallel"),
        name="attn_sample",
    )(qkv.reshape(b, t, nq), *views)


def _attn_out_kernel(*refs, merge, tm):
    if merge:
        o_refs = refs[:N_GROUPS]
        l_refs = refs[N_GROUPS:2 * N_GROUPS]
        x_ref, g_ref, wo_ref, out_ref = refs[2 * N_GROUPS:2 * N_GROUPS + 4]
        scr = refs[2 * N_GROUPS + 4:]
        vals = []
        for gi, ref in enumerate(o_refs + l_refs):
            dil = ATT_GROUPS[gi % N_GROUPS][1]
            if dil == 1:
                vals.append(ref[0, 0])
            else:
                for cb in range(ATT_WIDTH // LANES):
                    for res in range(dil):
                        scr[gi][cb, pl.ds(res, tm // dil, stride=dil), :] = (
                            ref[0, res, :, cb * LANES:(cb + 1) * LANES].astype(F32))
                vals.append(jnp.concatenate([scr[gi][cb] for cb in range(ATT_WIDTH // LANES)], axis=1))
        oa, ob, oc, la, lb, lc = vals
        mx = jnp.maximum(jnp.maximum(la, lb), lc)
        ea, eb, ec = jnp.exp(la - mx), jnp.exp(lb - mx), jnp.exp(lc - mx)
        den = ea + eb + ec
        o = (ea / den) * oa + (eb / den) * ob + (ec / den) * oc
    else:
        o0, x_ref, g_ref, wo_ref, out_ref = refs
        o = o0[...]
    y = _dot(o.astype(BF16), wo_ref[...])
    out_ref[...] = x_ref[...] + _rms(y, g_ref[...])


def _attn_out_call(os_, lses, x, g, wo, tm, batch=None):
    n, d = x.shape
    merge = lses is not None
    kern = functools.partial(_attn_out_kernel, merge=merge, tm=tm)
    if not merge:
        row = pl.BlockSpec((tm, d), lambda i: (i, 0))
        return pl.pallas_call(
            kern,
            grid=(n // tm,),
            in_specs=[pl.BlockSpec((tm, ATT_WIDTH), lambda i: (i, 0)), row,
                      pl.BlockSpec((1, d), lambda i: (0, 0)), pl.BlockSpec(wo.shape, lambda i: (0, 0))],
            out_specs=row,
            out_shape=jax.ShapeDtypeStruct((n, d), F32),
            compiler_params=_cparams("parallel"),
            name="attn_out_short",
        )(os_[0], x, g.reshape(1, d), wo)
    tpb = n // batch // tm
    row = pl.BlockSpec((tm, d), lambda b, j: (b * tpb + j, 0))
    aspecs = [pl.BlockSpec((1, dil, tm // dil, ATT_WIDTH), lambda b, j: (b, 0, j, 0))
              for _, dil in ATT_GROUPS]
    return pl.pallas_call(
        kern,
        grid=(batch, tpb),
        in_specs=aspecs * 2 + [row, pl.BlockSpec((1, d), lambda b, j: (0, 0)),
                               pl.BlockSpec(wo.shape, lambda b, j: (0, 0))],
        out_specs=row,
        out_shape=jax.ShapeDtypeStruct((n, d), F32),
        scratch_shapes=[pltpu.VMEM((ATT_WIDTH // LANES, tm, LANES), F32)] * (2 * N_GROUPS),
        compiler_params=_cparams("parallel", "parallel"),
        name="attn_out_long",
    )(*os_, *lses, x, g.reshape(1, d), wo)


def _pad_rows8(a):
    return jnp.concatenate([a, jnp.zeros((SUBLANES - a.shape[0],) + a.shape[1:], a.dtype)], axis=0)


def _pad_to(a, axis, size):
    padw = [(0, 0)] * a.ndim
    padw[axis] = (0, size - a.shape[axis])
    return jnp.pad(a, padw)


def _prep_weights(norm_g, rw, at_wqkv, at_wo, ff):
    (mu, wr, wk, wv, wo, w0, w1, w2, a0, a1, a2, g1, g2, k_k, k_a, r_k, lnx_g, lnx_b) = [p[0] for p in rw]
    glora = 2 * LANES
    sc = lambda a: _to_scan_heads(a.astype(BF16))
    proj_ws = [sc(wr), sc(wk), sc(wv),
               _pad_to(w1, 1, LANES).astype(BF16), _pad_to(sc(w2), 0, LANES),
               _pad_to(a1, 1, LANES).astype(BF16), _pad_to(sc(a2), 0, LANES),
               _pad_to(g1, 1, glora).astype(BF16), _pad_to(g2, 0, glora).astype(BF16)]
    return dict(
        mu8=_pad_rows8(mu),
        proj_vec=_pad_rows8(_to_scan_heads(jnp.stack([w0, a0, k_k, k_a, r_k.reshape(-1)]))),
        proj_ws=proj_ws,
        out_vec=_pad_rows8(jnp.stack([lnx_g, lnx_b, norm_g[0, 1]])),
        rw_wo=wo.astype(BF16),
        wqkv=at_wqkv[0].astype(BF16),
        at_wo=at_wo[0].astype(BF16),
        ff=[(_pad_rows8(jnp.stack([norm_g[l, 2], norm_g[l, 3]])),
             _pad_rows8(ff[1][l]), _pad_rows8(ff[2][l][None])) for l in range(2)],
        ff_wup=ff[0].astype(BF16),
        ff_wdown=ff[3].astype(BF16),
        ones=_seg_ones(),
    )


def _run_group(x, pos, shift0, wkv0, kv_bufs, conv0, norm_g, w, tm):
    b, t, d = x.shape
    n = b * t
    tm = min(tm, n)
    long_mode = shift0 is None
    xf = x.reshape(n, d)
    gn0 = norm_g[0, 0].reshape(1, d)
    to3 = lambda a: a.reshape(b, t, d)

    if long_mode:
        r, dec, k, kk, ka, g, bonus, v, hl = _rwkv_proj_call(xf, gn0, w, min(tm, 256), batch=b)
        new_shift = hl[:, SUBLANES - 1][None]
        s0 = jnp.zeros((b, SCAN_ROWS, LANES), F32)
    else:
        sh = jnp.concatenate([shift0[0][:, None, :], jnp.zeros((b, t - 1, d), F32)], axis=1)
        r, dec, k, kk, ka, g, bonus, v, hn = _rwkv_proj_call(
            xf, gn0, w, tm, seq_len=t, shift_rows=sh.reshape(n, d))
        new_shift = to3(hn)[:, -1][None]
        s0 = _state_to_scan(wkv0[0])
    y, s_fin = _scan_call(to3(kk), to3(dec), to3(k), to3(ka), to3(r), to3(v), s0,
                          min(t, SCAN_CHUNK))
    x1 = _rwkv_out_call(y.reshape(n, d), bonus, g, xf, w["out_vec"], w["rw_wo"], w["ones"],
                        min(t, OUT_TILE) if long_mode else tm)
    new_wkv = _state_from_scan(s_fin)[None]

    def ffn(xin, layer):
        g2, cw8, cb8 = w["ff"][layer]
        wup, wdown = w["ff_wup"], w["ff_wdown"]
        if long_mode:
            ftm = min(t, FF_LONG_TILE)
            xo, hl = _ffn_call(xin, g2, wup, cw8, cb8, wdown, ftm, layer, batch=b)
            cs = hl.reshape(b, t // ftm, SUBLANES, 2 * D_FF)[:, -1, SUBLANES - (CONV_W - 1):]
        else:
            st = conv0[layer]
            prev = _pad_to(st, 1, t).reshape(n, 2 * D_FF)
            xo, hl = _ffn_call(xin, g2, wup, cw8, cb8, wdown, tm, layer, seq_len=t, prev=prev)
            cs = hl.reshape(b, t, 2 * D_FF)[:, t - (CONV_W - 1):]
        return xo, cs

    x2, conv_a = ffn(x1, 0)

    new_kv = []
    if long_mode:
        a_g, kv_g = _qkv_call(x2, norm_g[1, 0], w["wqkv"], _rope_tables(pos), tm, batch=b)
        for kv in kv_g:
            new_kv.append(kv.reshape(1, b, kv.shape[1], 2, ATT_HEADS, ATT_HEAD_DIM))
        res = [_attn_prompt_call(a_g[gi], gi) for gi in range(N_GROUPS)]
        x3 = _attn_out_call([o for o, _ in res], [l for _, l in res], x2, norm_g[1, 1],
                            w["at_wo"], min(t, OUT_TILE), batch=b)
    else:
        tables = tuple(np.tile(a, (tm // t, 1)) for a in _rope_tables(pos))
        qkv = _qkv_call(x2, norm_g[1, 0], w["wqkv"], tables, tm)
        kvt = qkv.reshape(b, t, 3, N_GROUPS, ATT_HEADS, ATT_HEAD_DIM)[:, :, 1:]
        kvt = kvt.transpose(3, 1, 2, 4, 5, 0)
        for gi in range(N_GROUPS):
            new_kv.append(kvt[gi].transpose(4, 0, 1, 2, 3)[None])
        o = _attn_sample_call(qkv, [buf[0] for buf in kv_bufs], b, t)
        x3 = _attn_out_call([o.reshape(n, ATT_WIDTH)], None, x2, norm_g[1, 1], w["at_wo"], tm)
    x4, conv_b = ffn(x3, 1)
    return (x4.reshape(b, t, d), new_shift, new_wkv, new_kv, jnp.stack([conv_a, conv_b]))


def kernel(x_prompt, x_sample, state_shift, state_wkv, cache_kv_w128, cache_kv_w512, cache_kv_w2048, state_conv, norm_g, rw_mu, rw_wr, rw_wk, rw_wv, rw_wo, rw_w0, rw_w1, rw_w2, rw_a0, rw_a1, rw_a2, rw_g1, rw_g2, rw_kk, rw_ka, rw_rk, rw_lnx_g, rw_lnx_b, at_wqkv, at_wo, ff_wup, ff_conv_w, ff_conv_b, ff_wdown):
    rw = (rw_mu, rw_wr, rw_wk, rw_wv, rw_wo, rw_w0, rw_w1, rw_w2, rw_a0, rw_a1, rw_a2,
          rw_g1, rw_g2, rw_kk, rw_ka, rw_rk, rw_lnx_g, rw_lnx_b)
    ff = (ff_wup, ff_conv_w, ff_conv_b, ff_wdown)
    w = _prep_weights(norm_g, rw, at_wqkv, at_wo, ff)
    tp = x_prompt.shape[1]
    ts = x_sample.shape[1]
    pos_p = np.arange(tp)
    y_p, p_shift, p_wkv, p_kv, p_conv = _run_group(
        x_prompt, pos_p, None, None, None, None, norm_g, w, 512)
    pos_s = PAST_LEN + np.arange(ts)
    y_s, s_shift, s_wkv, s_kv, s_conv = _run_group(
        x_sample, pos_s, state_shift, state_wkv,
        (cache_kv_w128, cache_kv_w512, cache_kv_w2048), state_conv, norm_g, w, 256)
    return (y_p, y_s, p_shift, p_wkv, p_kv[0], p_kv[1], p_kv[2], p_conv,
            s_shift, s_wkv, s_kv[0], s_kv[1], s_kv[2], s_conv)
```
